```python
import math
import jax, jax.numpy as jnp
from jax import lax
import numpy as np

D_MODEL = 1024
BATCH = 16
SEQ = 2048
DEPTH = 1
DEC_BATCH = 128
DEC_SEQ = 1
PAST_LEN = 8192
PAGE_SIZE = 128

HEAD_DIM = 64
N_FOX_HEADS = 8
FOX_WIDTH = N_FOX_HEADS * HEAD_DIM
FORGET_BIAS_INIT = 3.0
N_MLA_HEADS = 8
MLA_NOPE_DIM = 64
MLA_ROPE_DIM = 32
MLA_V_DIM = 64
MLA_WIDTH = N_MLA_HEADS * MLA_V_DIM
Q_LORA_RANK = 384
KV_LORA_RANK = 256
LATENT_DIM = KV_LORA_RANK + MLA_ROPE_DIM
ROPE_THETA = 10000.0
MIX_WIDTH = FOX_WIDTH + MLA_WIDTH
D_FF = -(-8 * D_MODEL // (3 * 256)) * 256
PLE_DIM = 256
RMS_EPS = 1e-6
Q_BLOCK = 128
FOX_SCALE = HEAD_DIM ** -0.5
MLA_SCALE = (MLA_NOPE_DIM + MLA_ROPE_DIM) ** -0.5
IN_SPLITS = (FOX_WIDTH, FOX_WIDTH, FOX_WIDTH, N_FOX_HEADS, Q_LORA_RANK, KV_LORA_RANK, MLA_ROPE_DIM)
IN_WIDTH = sum(IN_SPLITS)
IN_OFFSETS = tuple(int(v) for v in np.cumsum(IN_SPLITS)[:-1])

kernel_name = "fox_mla_hybrid_decoder_step"


def rmsnorm(x, g):
    xf = x.astype(jnp.float32)
    y = xf * lax.rsqrt(jnp.mean(xf * xf, axis=-1, keepdims=True) + RMS_EPS)
    return (y * g.astype(jnp.float32)).astype(x.dtype)


def rope(x, pos):
    half = MLA_ROPE_DIM // 2
    inv = ROPE_THETA ** (-jnp.arange(half, dtype=jnp.float32) / half)
    ang = pos.astype(jnp.float32)[:, None] * inv[None, :]
    ang = ang.reshape(ang.shape[:1] + (1,) * (x.ndim - 3) + (half,))
    cos, sin = jnp.cos(ang), jnp.sin(ang)
    xf = x.astype(jnp.float32)
    x1, x2 = xf[..., :half], xf[..., half:]
    return jnp.concatenate([x1 * cos - x2 * sin, x1 * sin + x2 * cos], axis=-1).astype(x.dtype)


def block_attention(q, cq, q_pos, segments, scale, shared_kv):
    B, Tq, H, Dk = q.shape
    qb = min(Q_BLOCK, Tq)
    nb = -(-Tq // qb)
    pad = nb * qb - Tq
    q = q * jnp.asarray(scale, q.dtype)
    if pad:
        q = jnp.pad(q, ((0, 0), (0, pad), (0, 0), (0, 0)))
        q_pos = jnp.pad(q_pos, (0, pad), mode='edge')
        if cq is not None:
            cq = jnp.pad(cq, ((0, 0), (0, pad), (0, 0)), mode='edge')
    qs = q.reshape(B, nb, qb, H, Dk).swapaxes(0, 1)
    ps = q_pos.reshape(nb, qb)
    cs = None if cq is None else cq.reshape(B, nb, qb, H).transpose(1, 0, 3, 2)
    s_eq = 'bqhd,bkd->bhqk' if shared_kv else 'bqhd,bkhd->bhqk'
    o_eq = 'bhqk,bkd->bqhd' if shared_kv else 'bhqk,bkhd->bqhd'
    segs = [(k, v, kp, None if ck is None else jnp.swapaxes(ck, 1, 2)) for (k, v, kp, ck) in segments]

    def one_block(blk):
        qblk, pblk, cblk = blk
        scores = []
        for k, v, kp, ckT in segs:
            s = jnp.einsum(s_eq, qblk, k, preferred_element_type=jnp.float32)
            if ckT is not None:
                s = s + (cblk[:, :, :, None] - ckT[:, :, None, :])
            s = jnp.where(kp[None, :] <= pblk[:, None], s, -jnp.inf)
            scores.append(s)
        probs = jax.nn.softmax(jnp.concatenate(scores, axis=-1), axis=-1)
        out, off = None, 0
        for k, v, _, _ in segs:
            tk = k.shape[1]
            part = jnp.einsum(o_eq, probs[..., off:off + tk].astype(v.dtype), v)
            out = part if out is None else out + part
            off += tk
        return out

    outs = lax.map(one_block, (qs, ps, cs))
    Dv = outs.shape[-1]
    return outs.swapaxes(0, 1).reshape(B, nb * qb, H, Dv)[:, :Tq]


def trunk(x, p, pos, past, W):
    B, T, _ = x.shape
    h = x
    new_k, new_v, new_logf, new_lat = [], [], [], []
    for l in range(DEPTH):
        hn = rmsnorm(h, W['g_attn'][l])
        fq, fk, fv, f_logit, q_c, kv_c, k_r = jnp.split(hn @ W['w_in'][l], IN_OFFSETS, axis=-1)
        fq = fq.reshape(B, T, N_FOX_HEADS, HEAD_DIM)
        fk = fk.reshape(B, T, N_FOX_HEADS, HEAD_DIM)
        fv = fv.reshape(B, T, N_FOX_HEADS, HEAD_DIM)
        logf = jax.nn.log_sigmoid(f_logit.astype(jnp.float32) + W['b_f'][l].astype(jnp.float32))
        q = (rmsnorm(q_c, W['g_q'][l]) @ W['w_uq'][l]).reshape(B, T, N_MLA_HEADS, MLA_NOPE_DIM + MLA_ROPE_DIM)
        q_nope, q_rope = q[..., :MLA_NOPE_DIM], rope(q[..., MLA_NOPE_DIM:], pos)
        latent = jnp.concatenate([rmsnorm(kv_c, W['g_kv'][l]), rope(k_r, pos)], axis=-1)
        q_lat = jnp.concatenate([jnp.einsum('bthn,rhn->bthr', q_nope, W['w_uk'][l]), q_rope], axis=-1)

        c_new = jnp.cumsum(logf, axis=1)
        if past is None:
            fox_segs = [(fk, fv, pos, c_new)]
            mla_segs = [(latent, latent, pos, None)]
        else:
            ck, cv, clf, clat, pt = past
            n_past = pt.shape[1] * PAGE_SIZE
            pk = ck[l, pt].reshape(B, n_past, N_FOX_HEADS, HEAD_DIM)
            pv = cv[l, pt].reshape(B, n_past, N_FOX_HEADS, HEAD_DIM)
            c_past = jnp.cumsum(clf[l, pt].reshape(B, n_past, N_FOX_HEADS).astype(jnp.float32), axis=1)
            c_new = c_new + c_past[:, -1:, :]
            plat = clat[l, pt].reshape(B, n_past, LATENT_DIM)
            past_pos = jnp.arange(n_past, dtype=jnp.int32)
            fox_segs = [(pk, pv, past_pos, c_past), (fk, fv, pos, c_new)]
            mla_segs = [(plat, plat, past_pos, None), (latent, latent, pos, None)]

        fox_o = block_attention(fq, c_new, pos, fox_segs, FOX_SCALE, shared_kv=False)
        mla_lo = block_attention(q_lat, None, pos, mla_segs, MLA_SCALE, shared_kv=True)[..., :KV_LORA_RANK]
        mla_o = jnp.einsum('bthr,rhv->bthv', mla_lo, W['w_uv'][l])
        o = jnp.concatenate([rmsnorm(fox_o.reshape(B, T, FOX_WIDTH), W['g_fox_out'][l]),
                             rmsnorm(mla_o.reshape(B, T, MLA_WIDTH), W['g_mla_out'][l])], axis=-1)
        h = h + o @ W['w_o'][l]

        hn = rmsnorm(h, W['g_ffn'][l])
        h = h + (jax.nn.silu(hn @ W['w_gate'][l]) * (hn @ W['w_up'][l])) @ W['w_down'][l]

        gate = jax.nn.sigmoid(rmsnorm(h, W['g_ple'][l]) @ W['w_ple_gate'][l])
        h = h + gate * (p[l] @ W['w_ple_proj'][l])

        new_k.append(fk)
        new_v.append(fv)
        new_logf.append(logf)
        new_lat.append(latent)
    y = rmsnorm(h, W['g_final'])
    return y, jnp.stack(new_k), jnp.stack(new_v), jnp.stack(new_logf), jnp.stack(new_lat)


def setup_inputs(seed: int = 0) -> dict:
    key = jax.random.key(seed)
    ks = iter(jax.random.split(key, 40))
    f32 = jnp.float32

    def nrm(shape, fan_in):
        return jax.random.normal(next(ks), shape, f32) * fan_in ** -0.5

    def gain(shape):
        return 1.0 + 0.05 * jax.random.normal(next(ks), shape, f32)

    n_pages = PAST_LEN // PAGE_SIZE
    n_used = DEC_BATCH * n_pages
    n_pool = n_used + max(1, n_used // 4)
    page_table = jax.random.permutation(next(ks), n_pool)[:n_used].reshape(DEC_BATCH, n_pages).astype(jnp.int32)

    return {
        'x_prompt': jax.random.normal(next(ks), (BATCH, SEQ, D_MODEL), f32),
        'x_sample': jax.random.normal(next(ks), (DEC_BATCH, DEC_SEQ, D_MODEL), f32),
        'cache_fox_k': jax.random.normal(next(ks), (DEPTH, n_pool, PAGE_SIZE, N_FOX_HEADS, HEAD_DIM), f32),
        'cache_fox_v': jax.random.normal(next(ks), (DEPTH, n_pool, PAGE_SIZE, N_FOX_HEADS, HEAD_DIM), f32),
        'cache_fox_logf': jax.nn.log_sigmoid(FORGET_BIAS_INIT + jax.random.normal(next(ks), (DEPTH, n_pool, PAGE_SIZE, N_FOX_HEADS), f32)),
        'cache_mla_latent': jax.random.normal(next(ks), (DEPTH, n_pool, PAGE_SIZE, LATENT_DIM), f32),
        'page_table': page_table,
        'p_prompt': jax.random.normal(next(ks), (DEPTH, BATCH, SEQ, PLE_DIM), f32),
        'p_sample': jax.random.normal(next(ks), (DEPTH, DEC_BATCH, DEC_SEQ, PLE_DIM), f32),
        'g_attn': gain((DEPTH, D_MODEL)),
        'w_in': nrm((DEPTH, D_MODEL, IN_WIDTH), D_MODEL),
        'b_f': FORGET_BIAS_INIT + 0.1 * jax.random.normal(next(ks), (DEPTH, N_FOX_HEADS), f32),
        'g_q': gain((DEPTH, Q_LORA_RANK)),
        'w_uq': nrm((DEPTH, Q_LORA_RANK, N_MLA_HEADS * (MLA_NOPE_DIM + MLA_ROPE_DIM)), Q_LORA_RANK),
        'g_kv': gain((DEPTH, KV_LORA_RANK)),
        'w_uk': nrm((DEPTH, KV_LORA_RANK, N_MLA_HEADS, MLA_NOPE_DIM), KV_LORA_RANK),
        'w_uv': nrm((DEPTH, KV_LORA_RANK, N_MLA_HEADS, MLA_V_DIM), KV_LORA_RANK),
        'g_fox_out': gain((DEPTH, FOX_WIDTH)),
        'g_mla_out': gain((DEPTH, MLA_WIDTH)),
        'w_o': nrm((DEPTH, MIX_WIDTH, D_MODEL), MIX_WIDTH),
        'g_ffn': gain((DEPTH, D_MODEL)),
        'w_gate': nrm((DEPTH, D_MODEL, D_FF), D_MODEL),
        'w_up': nrm((DEPTH, D_MODEL, D_FF), D_MODEL),
        'w_down': nrm((DEPTH, D_FF, D_MODEL), D_FF),
        'g_ple': gain((DEPTH, D_MODEL)),
        'w_ple_gate': nrm((DEPTH, D_MODEL, D_MODEL), D_MODEL),
        'w_ple_proj': nrm((DEPTH, PLE_DIM, D_MODEL), PLE_DIM),
        'g_final': gain((D_MODEL,)),
    }


def reference(x_prompt, x_sample, cache_fox_k, cache_fox_v, cache_fox_logf, cache_mla_latent, page_table,
              p_prompt, p_sample, g_attn, w_in, b_f, g_q, w_uq, g_kv, w_uk, w_uv, g_fox_out, g_mla_out,
              w_o, g_ffn, w_gate, w_up, w_down, g_ple, w_ple_gate, w_ple_proj, g_final):
    W = dict(g_attn=g_attn, w_in=w_in, b_f=b_f, g_q=g_q, w_uq=w_uq, g_kv=g_kv, w_uk=w_uk, w_uv=w_uv,
             g_fox_out=g_fox_out, g_mla_out=g_mla_out, w_o=w_o, g_ffn=g_ffn, w_gate=w_gate, w_up=w_up,
             w_down=w_down, g_ple=g_ple, w_ple_gate=w_ple_gate, w_ple_proj=w_ple_proj, g_final=g_final)
    pos_prompt = jnp.arange(x_prompt.shape[1], dtype=jnp.int32)
    past_len = page_table.shape[1] * PAGE_SIZE
    pos_sample = past_len + jnp.arange(x_sample.shape[1], dtype=jnp.int32)

    y_prompt, fk_p, fv_p, flf_p, lat_p = trunk(x_prompt, p_prompt, pos_prompt, None, W)
    past = (cache_fox_k, cache_fox_v, cache_fox_logf, cache_mla_latent, page_table)
    y_sample, fk_s, fv_s, flf_s, lat_s = trunk(x_sample, p_sample, pos_sample, past, W)
    return (y_prompt, y_sample, fk_p, fv_p, flf_p, lat_p, fk_s, fv_s, flf_s, lat_s)
```

```python
import functools

import jax
import jax.numpy as jnp
import numpy as np
from jax import lax
from jax.experimental import pallas as pl
from jax.experimental.pallas import tpu as pltpu

D_MODEL = 1024
HEAD_DIM = 64
N_HEADS = 8
FOX_WIDTH = N_HEADS * HEAD_DIM
NOPE = 64
ROPE = 32
Q_LORA = 384
KV_LORA = 256
LATENT = KV_LORA + ROPE
D_FF = 2816
PLE_DIM = 256
PAGE = 128
ROPE_THETA = 10000.0
RMS_EPS = 1e-6
FOX_SCALE = HEAD_DIM ** -0.5
MLA_SCALE = (NOPE + ROPE) ** -0.5
NEG_BIG = -1e30

C_FQ, C_FK, C_FV, C_QC, C_KV, C_MISC = 0, 512, 1024, 1536, 1920, 2176
IN_COLS = 2304
M_KR, M_KRSW, M_FL = 0, 32, 64
TAB_COLS = 640

F32 = jnp.float32
BF16 = jnp.bfloat16
VMEM_LIMIT = 56 * 1024 * 1024

NT = (((1,), (1,)), ((), ()))


def _rms(x, g):
    return x * lax.rsqrt(jnp.mean(x * x, axis=-1, keepdims=True) + RMS_EPS) * g


def _const_spec(shape):
    nd = len(shape)
    return pl.BlockSpec(shape, lambda *_: (0,) * nd, pipeline_mode=pl.Buffered(1))


def _params(*sem):
    return pltpu.CompilerParams(dimension_semantics=sem, vmem_limit_bytes=VMEM_LIMIT)


def _lane_cumsum(x):
    n = x.shape[-1]
    lane = lax.broadcasted_iota(jnp.int32, x.shape, x.ndim - 1)
    sh = 1
    while sh < n:
        x = x + jnp.where(lane >= sh, pltpu.roll(x, sh, axis=x.ndim - 1), 0.0)
        sh *= 2
    return x


def _in_kernel(x_ref, gat_ref, win_ref, bf_ref, gq_ref, gkv_ref, wuq_ref, wuk_ref, tab_ref,
               fq_ref, fkt_ref, fvt_ref, logft_ref, latt_ref, lattb_ref, qlat_ref):
    x = x_ref[...]
    hn = _rms(x, gat_ref[...]).astype(BF16)
    z = jnp.dot(hn, win_ref[...], preferred_element_type=F32)

    fq_ref[...] = (z[:, C_FQ:C_FQ + FOX_WIDTH] * FOX_SCALE).astype(BF16)
    fkt_ref[...] = z[:, C_FK:C_FK + FOX_WIDTH].T
    fvt_ref[...] = z[:, C_FV:C_FV + FOX_WIDTH].T

    misc = z[:, C_MISC:C_MISC + 128]
    t = misc + bf_ref[...]
    lf = jnp.minimum(t, 0.0) - jnp.log1p(jnp.exp(-jnp.abs(t)))
    lane = lax.broadcasted_iota(jnp.int32, misc.shape, 1)
    mt = jnp.where(lane < M_FL, misc * tab_ref[:, 512:640], lf).T
    k_rot_t = mt[M_KR:M_KR + ROPE] + mt[M_KRSW:M_KRSW + ROPE]
    logft_ref[...] = mt[M_FL:M_FL + N_HEADS]

    kvn_t = _rms(z[:, C_KV:C_KV + KV_LORA], gkv_ref[...]).T
    latt_ref[:KV_LORA, :] = kvn_t
    latt_ref[KV_LORA:, :] = k_rot_t
    lattb_ref[:KV_LORA, :] = kvn_t.astype(BF16)
    lattb_ref[KV_LORA:, :] = k_rot_t.astype(BF16)

    qn = _rms(z[:, C_QC:C_QC + Q_LORA], gq_ref[...]).astype(BF16)
    q = jnp.dot(qn, wuq_ref[...], preferred_element_type=F32)
    q_rot = q[:, 512:768] * tab_ref[:, 0:256] + q[:, 768:1024] * tab_ref[:, 256:512]
    q_rot = (q_rot * MLA_SCALE).astype(BF16)
    for h in range(N_HEADS):
        p = h // 2
        q_pair = q[:, p * 128:(p + 1) * 128].astype(BF16)
        ql = jnp.dot(q_pair, wuk_ref[h], preferred_element_type=F32)
        qlat_ref[h, :, :KV_LORA] = (ql * MLA_SCALE).astype(BF16)
        qlat_ref[h, :, KV_LORA:] = q_rot[:, h * ROPE:(h + 1) * ROPE]


def _in_proj(x, W, tab, b, t, bm):
    m = b * t
    nt = t // bm
    row = lambda w: pl.BlockSpec((bm, w), lambda i: (i, 0))
    tr = lambda w: pl.BlockSpec((None, w, bm), lambda i: (i // nt, 0, i % nt))
    out_shape = (
        jax.ShapeDtypeStruct((m, FOX_WIDTH), BF16),
        jax.ShapeDtypeStruct((b, FOX_WIDTH, t), F32),
        jax.ShapeDtypeStruct((b, FOX_WIDTH, t), F32),
        jax.ShapeDtypeStruct((b, N_HEADS, t), F32),
        jax.ShapeDtypeStruct((b, LATENT, t), F32),
        jax.ShapeDtypeStruct((b, LATENT, t), BF16),
        jax.ShapeDtypeStruct((N_HEADS, m, LATENT), BF16),
    )
    out_specs = (row(FOX_WIDTH), tr(FOX_WIDTH), tr(FOX_WIDTH), tr(N_HEADS), tr(LATENT), tr(LATENT),
                 pl.BlockSpec((N_HEADS, bm, LATENT), lambda i: (0, i, 0)))
    in_specs = [
        row(D_MODEL), _const_spec((1, D_MODEL)), _const_spec((D_MODEL, IN_COLS)), _const_spec((1, 128)),
        _const_spec((1, Q_LORA)), _const_spec((1, KV_LORA)), _const_spec((Q_LORA, 1024)),
        _const_spec((N_HEADS, 128, KV_LORA)), pl.BlockSpec((bm, TAB_COLS), lambda i: (i % nt, 0)),
    ]
    return pl.pallas_call(
        _in_kernel, grid=(m // bm,), in_specs=in_specs, out_specs=out_specs, out_shape=out_shape,
        compiler_params=_params("parallel"), name="in_proj",
    )(x, W["g_attn"], W["w_in"], W["b_f"], W["g_q"], W["g_kv"], W["w_uq"], W["w_uk"], tab)


def _cumsum_kernel(x_ref, o_ref):
    c = _lane_cumsum(x_ref[...])
    for h in range(N_HEADS):
        o_ref[h] = c[h:h + 1, :]


def _cumsum(logft):
    b, _, t = logft.shape
    return pl.pallas_call(
        _cumsum_kernel, grid=(b,),
        in_specs=[pl.BlockSpec((None, N_HEADS, t), lambda i: (i, 0, 0))],
        out_specs=pl.BlockSpec((None, N_HEADS, 1, t), lambda i: (i, 0, 0, 0)),
        out_shape=jax.ShapeDtypeStruct((b, N_HEADS, 1, t), F32),
        compiler_params=_params("parallel"), name="logf_cumsum",
    )(logft)


def _softmax_step(s, pv, m, l, acc):
    m_new = jnp.maximum(m, jnp.max(s, axis=1, keepdims=True))
    alpha = jnp.exp(m - m_new)
    p = jnp.exp(s - m_new)
    l = alpha * l + jnp.sum(p, axis=1, keepdims=True)
    acc = alpha * acc + pv(p)
    return m_new, l, acc


def _causal_mask(s, tq):
    row = lax.broadcasted_iota(jnp.int32, s.shape, 0) % tq
    col = lax.broadcasted_iota(jnp.int32, s.shape, 1)
    return jnp.where(col <= row, s, NEG_BIG)


def _fox_kernel(q_ref, kt_ref, vt_ref, ck_ref, o_ref, *, tq):
    qi = pl.program_id(2)
    q = q_ref[...].astype(F32)
    lane = lax.broadcasted_iota(jnp.int32, q.shape, 1)
    q2 = jnp.concatenate([jnp.where(lane < HEAD_DIM, q, 0.0), jnp.where(lane >= HEAD_DIM, q, 0.0)], axis=0)
    r = 2 * tq

    def block(j, carry, diagonal):
        off = pl.multiple_of(j * tq, tq)
        s = jnp.dot(q2, kt_ref[:, pl.ds(off, tq)], preferred_element_type=F32)
        s = (s.reshape(2, tq, tq) - ck_ref[:, :, pl.ds(off, tq)]).reshape(r, tq)
        if diagonal:
            s = _causal_mask(s, tq)
        pv = lambda p: lax.dot_general(p, vt_ref[:, pl.ds(off, tq)], NT, preferred_element_type=F32)
        return _softmax_step(s, pv, *carry)

    init = (jnp.full((r, 1), NEG_BIG, F32), jnp.zeros((r, 1), F32), jnp.zeros((r, 128), F32))
    carry = lax.fori_loop(0, qi, lambda j, c: block(j, c, False), init)
    m, l, acc = block(qi, carry, True)
    o = acc / l
    o_ref[...] = jnp.where(lane < HEAD_DIM, o[:tq], o[tq:])


def _fox_attention(fq, fkt, fvt, ck, tq):
    b, _, t = fkt.shape
    nq = t // tq
    kv = pl.BlockSpec((None, 128, t), lambda bi, p, qi: (bi, p, 0))
    return pl.pallas_call(
        functools.partial(_fox_kernel, tq=tq), grid=(b, N_HEADS // 2, nq),
        in_specs=[pl.BlockSpec((tq, 128), lambda bi, p, qi: (bi * nq + qi, p)), kv, kv,
                  pl.BlockSpec((None, 2, 1, t), lambda bi, p, qi: (bi, p, 0, 0))],
        out_specs=pl.BlockSpec((tq, 128), lambda bi, p, qi: (bi * nq + qi, p)),
        out_shape=jax.ShapeDtypeStruct((b * t, FOX_WIDTH), F32),
        compiler_params=_params("parallel", "parallel", "arbitrary"), name="fox_attention",
    )(fq, fkt, fvt, ck)


def _mla_kernel(q_ref, latt_ref, o_ref, *, tq):
    qi = pl.program_id(1)
    r = N_HEADS * tq
    q = q_ref[...].reshape(r, LATENT)

    def block(j, carry, diagonal):
        off = pl.multiple_of(j * tq, tq)
        s = jnp.dot(q, latt_ref[:, pl.ds(off, tq)], preferred_element_type=F32)
        if diagonal:
            s = _causal_mask(s, tq)
        pv = lambda p: lax.dot_general(p.astype(BF16), latt_ref[:KV_LORA, pl.ds(off, tq)], NT,
                                       preferred_element_type=F32)
        return _softmax_step(s, pv, *carry)

    init = (jnp.full((r, 1), NEG_BIG, F32), jnp.zeros((r, 1), F32), jnp.zeros((r, KV_LORA), F32))
    carry = lax.fori_loop(0, qi, lambda j, c: block(j, c, False), init)
    m, l, acc = block(qi, carry, True)
    o_ref[...] = (acc / l).astype(BF16).reshape(N_HEADS, tq, KV_LORA)


def _mla_attention(qlat, lattb, tq):
    b, _, t = lattb.shape
    nq = t // tq
    return pl.pallas_call(
        functools.partial(_mla_kernel, tq=tq), grid=(b, nq),
        in_specs=[pl.BlockSpec((N_HEADS, tq, LATENT), lambda bi, qi: (0, bi * nq + qi, 0)),
                  pl.BlockSpec((None, LATENT, t), lambda bi, qi: (bi, 0, 0))],
        out_specs=pl.BlockSpec((N_HEADS, tq, KV_LORA), lambda bi, qi: (0, bi * nq + qi, 0)),
        out_shape=jax.ShapeDtypeStruct((N_HEADS, b * t, KV_LORA), BF16),
        compiler_params=_params("parallel", "arbitrary"), name="mla_attention",
    )(qlat, lattb)


def _decode_kernel(pt_ref, fq_ref, ql_ref, kn_ref, vn_ref, ln_ref, lfn_ref, *rest, pages):
    kt_refs = rest[0 * pages:1 * pages]
    vt_refs = rest[1 * pages:2 * pages]
    lt_refs = rest[2 * pages:3 * pages]
    ft_refs = rest[3 * pages:4 * pages]
    fo_ref, mo_ref = rest[4 * pages:4 * pages + 2]
    mf, lf, af, mm, lm, am, coff = rest[4 * pages + 2:]
    c = pl.program_id(1)
    nc = pl.num_programs(1)

    @pl.when(c == 0)
    def _():
        mf[...] = jnp.full(mf.shape, NEG_BIG, F32)
        mm[...] = jnp.full(mm.shape, NEG_BIG, F32)
        for ref in (lf, af, lm, am, coff):
            ref[...] = jnp.zeros(ref.shape, F32)

    hrow = lax.broadcasted_iota(jnp.int32, (N_HEADS, FOX_WIDTH), 0)
    hcol = lax.broadcasted_iota(jnp.int32, (N_HEADS, FOX_WIDTH), 1) // HEAD_DIM
    own = hrow == hcol
    qbd = jnp.where(own, fq_ref[0].astype(F32), 0.0)
    qlat = ql_ref[0].astype(F32)
    cat = lambda refs: jnp.concatenate([r_[0] for r_ in refs], axis=1)

    cum = _lane_cumsum(cat(ft_refs)) + coff[:, 0:1]
    tk = cum.shape[1]
    c_last = cum[:, tk - 1:tk]
    coff[:, 0:1] = c_last

    kt = cat(kt_refs)
    vt = cat(vt_refs)
    s = jnp.dot(qbd, kt, preferred_element_type=F32) - cum
    pv = lambda p: lax.dot_general(p, vt, NT, preferred_element_type=F32)
    m_new, l_new, a_new = _softmax_step(s, pv, mf[:, 0:1], lf[:, 0:1], af[...])

    lt = cat(lt_refs)
    s2 = jnp.dot(qlat, lt, preferred_element_type=F32)
    pv2 = lambda p: lax.dot_general(p, lt[:KV_LORA], NT, preferred_element_type=F32)
    m2, l2, a2 = _softmax_step(s2, pv2, mm[:, 0:1], lm[:, 0:1], am[...])

    @pl.when(c < nc - 1)
    def _():
        mf[:, 0:1] = m_new
        lf[:, 0:1] = l_new
        af[...] = a_new
        mm[:, 0:1] = m2
        lm[:, 0:1] = l2
        am[...] = a2

    @pl.when(c == nc - 1)
    def _():
        c_tot = c_last + lfn_ref[0]
        sn = jnp.sum(qbd * kn_ref[0], axis=1, keepdims=True) - c_tot
        mfin = jnp.maximum(m_new, sn)
        al = jnp.exp(m_new - mfin)
        pn = jnp.exp(sn - mfin)
        o = (al * a_new + pn * vn_ref[0]) / (al * l_new + pn)
        fo_ref[0] = jnp.sum(jnp.where(own, o, 0.0), axis=0, keepdims=True)

        ln = ln_ref[0]
        sn2 = jnp.sum(qlat * ln, axis=1, keepdims=True)
        mfin2 = jnp.maximum(m2, sn2)
        al2 = jnp.exp(m2 - mfin2)
        pn2 = jnp.exp(sn2 - mfin2)
        mo_ref[0] = ((al2 * a2 + pn2 * ln[:, :KV_LORA]) / (al2 * l2 + pn2)).astype(BF16)


def _decode_attention(page_table, fq, qlat_t, fk_new, fv_new, lat_new, lf_new, ckt, cvt, clt, cft, pages):
    nb, n_pages = page_table.shape
    nc = n_pages // pages
    pt = page_table.reshape(-1)

    def page_spec(rows, i):
        return pl.BlockSpec((1, rows, PAGE), lambda b, c, pt_: (pt_[b * n_pages + c * pages + i], 0, 0))

    seq = lambda *shape: pl.BlockSpec((1,) + shape, lambda b, c, pt_: (b,) + (0,) * len(shape))
    in_specs = [seq(1, FOX_WIDTH), seq(N_HEADS, LATENT), seq(1, FOX_WIDTH), seq(1, FOX_WIDTH),
                seq(1, LATENT), seq(N_HEADS, 1)]
    operands = [fq.reshape(nb, 1, FOX_WIDTH), qlat_t, fk_new.reshape(nb, 1, FOX_WIDTH),
                fv_new.reshape(nb, 1, FOX_WIDTH), lat_new.reshape(nb, 1, LATENT), lf_new.reshape(nb, N_HEADS, 1)]
    for arr, rows in ((ckt, FOX_WIDTH), (cvt, FOX_WIDTH), (clt, LATENT), (cft, N_HEADS)):
        for i in range(pages):
            in_specs.append(page_spec(rows, i))
            operands.append(arr)
    scratch = [pltpu.VMEM((N_HEADS, 128), F32), pltpu.VMEM((N_HEADS, 128), F32), pltpu.VMEM((N_HEADS, FOX_WIDTH), F32),
               pltpu.VMEM((N_HEADS, 128), F32), pltpu.VMEM((N_HEADS, 128), F32), pltpu.VMEM((N_HEADS, KV_LORA), F32),
               pltpu.VMEM((N_HEADS, 128), F32)]
    grid_spec = pltpu.PrefetchScalarGridSpec(
        num_scalar_prefetch=1, grid=(nb, nc), in_specs=in_specs,
        out_specs=(seq(1, FOX_WIDTH), seq(N_HEADS, KV_LORA)),
        scratch_shapes=scratch)
    return pl.pallas_call(
        functools.partial(_decode_kernel, pages=pages), grid_spec=grid_spec,
        out_shape=(jax.ShapeDtypeStruct((nb, 1, FOX_WIDTH), F32), jax.ShapeDtypeStruct((nb, N_HEADS, KV_LORA), BF16)),
        compiler_params=_params("parallel", "arbitrary"), name="decode_attention",
    )(pt, *operands)


def _post_kernel(x_ref, fo_ref, ml_ref, wuv_ref, gfo_ref, gmo_ref, wo_ref, h_ref):
    fo = _rms(fo_ref[...], gfo_ref[...]).astype(BF16)
    mo = jnp.concatenate(
        [jnp.dot(ml_ref[h], wuv_ref[h], preferred_element_type=F32) for h in range(N_HEADS)], axis=1)
    mo = _rms(mo, gmo_ref[...]).astype(BF16)
    o = jnp.dot(fo, wo_ref[:FOX_WIDTH, :], preferred_element_type=F32)
    o = o + jnp.dot(mo, wo_ref[FOX_WIDTH:, :], preferred_element_type=F32)
    h_ref[...] = x_ref[...] + o


def _post(x, fox_o, mla_lo, W, bm):
    m = x.shape[0]
    row = lambda w_: pl.BlockSpec((bm, w_), lambda i: (i, 0))
    return pl.pallas_call(
        _post_kernel, grid=(m // bm,),
        in_specs=[row(D_MODEL), row(FOX_WIDTH), pl.BlockSpec((N_HEADS, bm, KV_LORA), lambda i: (0, i, 0)),
                  _const_spec((N_HEADS, KV_LORA, HEAD_DIM)), _const_spec((1, FOX_WIDTH)), _const_spec((1, FOX_WIDTH)),
                  _const_spec((D_MODEL, D_MODEL))],
        out_specs=row(D_MODEL), out_shape=jax.ShapeDtypeStruct((m, D_MODEL), F32),
        compiler_params=_params("parallel"), name="attn_out_proj",
    )(x, fox_o, mla_lo, W["w_uv"], W["g_fox_out"], W["g_mla_out"], W["w_o"])


def _ffn_kernel(h_ref, p_ref, gffn_ref, wg_ref, wu_ref, wd_ref, gple_ref, wpg_ref, wpp_ref, gfin_ref, y_ref):
    h = h_ref[...]
    hn = _rms(h, gffn_ref[...]).astype(BF16)
    g = jnp.dot(hn, wg_ref[...], preferred_element_type=F32)
    u = jnp.dot(hn, wu_ref[...], preferred_element_type=F32)
    act = (g * (1.0 / (1.0 + jnp.exp(-g))) * u).astype(BF16)
    h = h + jnp.dot(act, wd_ref[...], preferred_element_type=F32)

    hp = _rms(h, gple_ref[...]).astype(BF16)
    zg = jnp.dot(hp, wpg_ref[...], preferred_element_type=F32)
    gate = 1.0 / (1.0 + jnp.exp(-zg))
    proj = jnp.dot(p_ref[...].astype(BF16), wpp_ref[...], preferred_element_type=F32)
    h = h + gate * proj
    y_ref[...] = _rms(h, gfin_ref[...])


def _ffn(h, p, W, bm):
    m = h.shape[0]
    row = lambda w_: pl.BlockSpec((bm, w_), lambda i: (i, 0))
    return pl.pallas_call(
        _ffn_kernel, grid=(m // bm,),
        in_specs=[row(D_MODEL), row(PLE_DIM), _const_spec((1, D_MODEL)), _const_spec((D_MODEL, D_FF)),
                  _const_spec((D_MODEL, D_FF)), _const_spec((D_FF, D_MODEL)), _const_spec((1, D_MODEL)),
                  _const_spec((D_MODEL, D_MODEL)), _const_spec((PLE_DIM, D_MODEL)), _const_spec((1, D_MODEL))],
        out_specs=row(D_MODEL), out_shape=jax.ShapeDtypeStruct((m, D_MODEL), F32),
        compiler_params=_params("parallel"), name="ffn_ple_norm",
    )(h, p, W["g_ffn"], W["w_gate"], W["w_up"], W["w_down"], W["g_ple"], W["w_ple_gate"], W["w_ple_proj"],
      W["g_final"])


def _prep_weights(g_attn, w_in, b_f, g_q, w_uq, g_kv, w_uk, w_uv, g_fox_out, g_mla_out, w_o, g_ffn, w_gate, w_up,
                  w_down, g_ple, w_ple_gate, w_ple_proj, g_final):
    w = w_in[0]
    o = np.cumsum((0, 512, 512, 512, N_HEADS, Q_LORA, KV_LORA, ROPE))
    fq, fk, fv, fl, qc, kv, kr = (w[:, o[i]:o[i + 1]] for i in range(7))
    kr_sw = jnp.concatenate([kr[:, ROPE // 2:], kr[:, :ROPE // 2]], axis=1)
    misc = jnp.concatenate([kr, kr_sw, fl, jnp.zeros((D_MODEL, 128 - 2 * ROPE - N_HEADS), F32)], axis=1)
    w_in_p = jnp.concatenate([fq, fk, fv, qc, kv, misc], axis=1).astype(BF16)

    uq = w_uq[0].reshape(Q_LORA, N_HEADS, NOPE + ROPE)
    uq_nope = uq[:, :, :NOPE].reshape(Q_LORA, N_HEADS * NOPE)
    uq_rope = uq[:, :, NOPE:]
    uq_rope_sw = jnp.concatenate([uq_rope[:, :, ROPE // 2:], uq_rope[:, :, :ROPE // 2]], axis=2)
    w_uq_p = jnp.concatenate([uq_nope, uq_rope.reshape(Q_LORA, -1), uq_rope_sw.reshape(Q_LORA, -1)], axis=1).astype(BF16)

    uk = jnp.transpose(w_uk[0], (1, 2, 0))
    zeros = jnp.zeros_like(uk)
    even = jnp.concatenate([uk, zeros], axis=1)
    odd = jnp.concatenate([zeros, uk], axis=1)
    w_uk_p = jnp.where((jnp.arange(N_HEADS) % 2 == 0)[:, None, None], even, odd).astype(BF16)

    b_row = jnp.zeros((1, 128), F32).at[0, M_FL:M_FL + N_HEADS].set(b_f[0])
    return dict(
        g_attn=g_attn, w_in=w_in_p, b_f=b_row, g_q=g_q, g_kv=g_kv, w_uq=w_uq_p, w_uk=w_uk_p,
        w_uv=jnp.transpose(w_uv[0], (1, 0, 2)).astype(BF16), g_fox_out=g_fox_out, g_mla_out=g_mla_out,
        w_o=w_o[0].astype(BF16), g_ffn=g_ffn, w_gate=w_gate[0].astype(BF16), w_up=w_up[0].astype(BF16),
        w_down=w_down[0].astype(BF16), g_ple=g_ple, w_ple_gate=w_ple_gate[0].astype(BF16),
        w_ple_proj=w_ple_proj[0].astype(BF16), g_final=g_final.reshape(1, D_MODEL))


def _rope_tables(pos):
    half = ROPE // 2
    inv = ROPE_THETA ** (-jnp.arange(half, dtype=F32) / half)
    ang = pos.astype(F32)[:, None] * inv[None, :]
    cos, sin = jnp.cos(ang), jnp.sin(ang)
    c32 = jnp.concatenate([cos, cos], axis=1)
    s32 = jnp.concatenate([-sin, sin], axis=1)
    pad = jnp.zeros((pos.shape[0], 128 - 2 * ROPE), F32)
    return jnp.concatenate([jnp.tile(c32, (1, N_HEADS)), jnp.tile(s32, (1, N_HEADS)), c32, s32, pad], axis=1)


def _block(m, target):
    b = min(m, target)
    while m % b:
        b //= 2
    return b


def kernel(x_prompt, x_sample, cache_fox_k, cache_fox_v, cache_fox_logf, cache_mla_latent, page_table, p_prompt, p_sample, g_attn, w_in, b_f, g_q, w_uq, g_kv, w_uk, w_uv, g_fox_out, g_mla_out, w_o, g_ffn, w_gate, w_up, w_down, g_ple, w_ple_gate, w_ple_proj, g_final):
    assert w_in.shape[0] == 1, "one layer"
    W = _prep_weights(g_attn, w_in, b_f, g_q, w_uq, g_kv, w_uk, w_uv, g_fox_out, g_mla_out, w_o, g_ffn, w_gate,
                      w_up, w_down, g_ple, w_ple_gate, w_ple_proj, g_final)
    b, t, _ = x_prompt.shape
    nb, ts, _ = x_sample.shape
    assert ts == 1
    n_pool = cache_fox_k.shape[1]
    past_len = page_table.shape[1] * PAGE

    def heads_last(xt, lead):
        return jnp.transpose(xt.reshape(lead, N_HEADS, HEAD_DIM, -1), (0, 3, 1, 2))

    mp = b * t
    xp = x_prompt.reshape(mp, D_MODEL)
    tq = _block(t, 256)
    fq, fkt, fvt, logft, latt, lattb, qlat = _in_proj(xp, W, _rope_tables(jnp.arange(t, dtype=jnp.int32)), b, t, tq)
    ck = _cumsum(logft)
    fox_o = _fox_attention(fq, fkt, fvt, ck, tq)
    mla_lo = _mla_attention(qlat, lattb, tq)
    h1 = _post(xp, fox_o, mla_lo, W, _block(mp, 256))
    y_p = _ffn(h1, p_prompt.reshape(mp, PLE_DIM), W, _block(mp, 256))

    xs = x_sample.reshape(nb, D_MODEL)
    tab_s = _rope_tables(jnp.full((nb,), past_len, jnp.int32))
    fq_s, fkt_s, fvt_s, logft_s, latt_s, _, qlat_s = _in_proj(xs, W, tab_s, 1, nb, nb)
    fox_s, mla_s = _decode_attention(
        page_table, fq_s, jnp.transpose(qlat_s, (1, 0, 2)), fkt_s[0].T, fvt_s[0].T, latt_s[0].T, logft_s[0].T,
        jnp.transpose(cache_fox_k[0], (0, 2, 3, 1)).reshape(n_pool, FOX_WIDTH, PAGE),
        jnp.transpose(cache_fox_v[0], (0, 2, 3, 1)).reshape(n_pool, FOX_WIDTH, PAGE),
        jnp.transpose(cache_mla_latent[0], (0, 2, 1)), jnp.transpose(cache_fox_logf[0], (0, 2, 1)),
        pages=_block(page_table.shape[1], 8))
    h1s = _post(xs, fox_s.reshape(nb, FOX_WIDTH), jnp.transpose(mla_s, (1, 0, 2)), W, nb)
    y_s = _ffn(h1s, p_sample.reshape(nb, PLE_DIM), W, nb)

    return (
        y_p.reshape(b, t, D_MODEL), y_s.reshape(nb, 1, D_MODEL),
        heads_last(fkt, b)[None], heads_last(fvt, b)[None],
        jnp.transpose(logft, (0, 2, 1))[None], jnp.transpose(latt, (0, 2, 1))[None],
        heads_last(fkt_s, 1).reshape(1, nb, 1, N_HEADS, HEAD_DIM), heads_last(fvt_s, 1).reshape(1, nb, 1, N_HEADS, HEAD_DIM),
        jnp.transpose(logft_s, (0, 2, 1)).reshape(1, nb, 1, N_HEADS),
        jnp.transpose(latt_s, (0, 2, 1)).reshape(1, nb, 1, LATENT),
    )
```

```python
import functools

import jax
import jax.numpy as jnp
import numpy as np
from jax import lax
from jax.experimental import pallas as pl
from jax.experimental.pallas import tpu as pltpu

D_MODEL = 1024
HEAD_DIM = 64
N_HEADS = 8
FOX_WIDTH = N_HEADS * HEAD_DIM
NOPE = 64
ROPE = 32
Q_LORA = 384
KV_LORA = 256
LATENT = KV_LORA + ROPE
D_FF = 2816
PLE_DIM = 256
PAGE = 128
ROPE_THETA = 10000.0
RMS_EPS = 1e-6
FOX_SCALE = HEAD_DIM ** -0.5
MLA_SCALE = (NOPE + ROPE) ** -0.5
NEG_BIG = -1e30

C_FQ, C_FK, C_FV, C_QC, C_KV, C_MISC = 0, 512, 1024, 1536, 1920, 2176
IN_COLS = 2304
M_KR, M_KRSW, M_FL = 0, 32, 64
TAB_COLS = 640
CP_COLS = 32

F32 = jnp.float32
BF16 = jnp.bfloat16
VMEM_LIMIT = 56 * 1024 * 1024

NT = (((1,), (1,)), ((), ()))


def _rms(x, g):
    return x * lax.rsqrt(jnp.mean(x * x, axis=-1, keepdims=True) + RMS_EPS) * g


def _const_spec(shape):
    nd = len(shape)
    return pl.BlockSpec(shape, lambda *_: (0,) * nd, pipeline_mode=pl.Buffered(1))


def _params(*sem):
    return pltpu.CompilerParams(dimension_semantics=sem, vmem_limit_bytes=VMEM_LIMIT)


def _lane_cumsum(x):
    n = x.shape[-1]
    lane = lax.broadcasted_iota(jnp.int32, x.shape, x.ndim - 1)
    sh = 1
    while sh < n:
        x = x + jnp.where(lane >= sh, pltpu.roll(x, sh, axis=x.ndim - 1), 0.0)
        sh *= 2
    return x


def _in_kernel(x_ref, gat_ref, win_ref, bf_ref, gq_ref, gkv_ref, wuq_ref, wuk_ref, tab_ref,
               fq_ref, fk_ref, fkt_ref, fvt_ref, logft_ref, latt_ref, lattb_ref, latr_ref, qlat_ref):
    x = x_ref[...]
    hn = _rms(x, gat_ref[...]).astype(BF16)
    z = jnp.dot(hn, win_ref[...], preferred_element_type=F32)

    fq_ref[...] = (z[:, C_FQ:C_FQ + FOX_WIDTH] * FOX_SCALE).astype(BF16)
    fk_ref[...] = z[:, C_FK:C_FK + FOX_WIDTH]
    fkt_ref[...] = z[:, C_FK:C_FK + FOX_WIDTH].T
    fvt_ref[...] = z[:, C_FV:C_FV + FOX_WIDTH].T

    misc = z[:, C_MISC:C_MISC + 128]
    t = misc + bf_ref[...]
    lf = jnp.minimum(t, 0.0) - jnp.log1p(jnp.exp(-jnp.abs(t)))
    lane = lax.broadcasted_iota(jnp.int32, misc.shape, 1)
    rot = misc * tab_ref[:, 512:640]
    mt = jnp.where(lane < M_FL, rot, lf).T
    k_rot_t = mt[M_KR:M_KR + ROPE] + mt[M_KRSW:M_KRSW + ROPE]
    logft_ref[...] = mt[M_FL:M_FL + N_HEADS]

    kvn = _rms(z[:, C_KV:C_KV + KV_LORA], gkv_ref[...])
    kvn_t = kvn.T
    latt_ref[:KV_LORA, :] = kvn_t
    latt_ref[KV_LORA:, :] = k_rot_t
    lattb_ref[:KV_LORA, :] = kvn_t.astype(BF16)
    lattb_ref[KV_LORA:, :] = k_rot_t.astype(BF16)
    latr_ref[:, :KV_LORA] = kvn.astype(BF16)
    latr_ref[:, KV_LORA:] = (rot[:, M_KR:M_KR + ROPE] + rot[:, M_KRSW:M_KRSW + ROPE]).astype(BF16)

    qn = _rms(z[:, C_QC:C_QC + Q_LORA], gq_ref[...]).astype(BF16)
    q = jnp.dot(qn, wuq_ref[...], preferred_element_type=F32)
    q_rot = q[:, 512:768] * tab_ref[:, 0:256] + q[:, 768:1024] * tab_ref[:, 256:512]
    q_rot = (q_rot * MLA_SCALE).astype(BF16)
    for h in range(N_HEADS):
        p = h // 2
        q_pair = q[:, p * 128:(p + 1) * 128].astype(BF16)
        ql = jnp.dot(q_pair, wuk_ref[h], preferred_element_type=F32)
        qlat_ref[h, :, :KV_LORA] = (ql * MLA_SCALE).astype(BF16)
        qlat_ref[h, :, KV_LORA:] = q_rot[:, h * ROPE:(h + 1) * ROPE]


def _in_proj(x, W, tab, b, t, bm):
    m = b * t
    nt = t // bm
    row = lambda w: pl.BlockSpec((bm, w), lambda i: (i, 0))
    tr = lambda w: pl.BlockSpec((None, w, bm), lambda i: (i // nt, 0, i % nt))
    out_shape = (
        jax.ShapeDtypeStruct((m, FOX_WIDTH), BF16),
        jax.ShapeDtypeStruct((m, FOX_WIDTH), F32),
        jax.ShapeDtypeStruct((b, FOX_WIDTH, t), F32),
        jax.ShapeDtypeStruct((b, FOX_WIDTH, t), F32),
        jax.ShapeDtypeStruct((b, N_HEADS, t), F32),
        jax.ShapeDtypeStruct((b, LATENT, t), F32),
        jax.ShapeDtypeStruct((b, LATENT, t), BF16),
        jax.ShapeDtypeStruct((m, LATENT), BF16),
        jax.ShapeDtypeStruct((N_HEADS, m, LATENT), BF16),
    )
    out_specs = (row(FOX_WIDTH), row(FOX_WIDTH), tr(FOX_WIDTH), tr(FOX_WIDTH), tr(N_HEADS), tr(LATENT), tr(LATENT),
                 row(LATENT), pl.BlockSpec((N_HEADS, bm, LATENT), lambda i: (0, i, 0)))
    in_specs = [
        row(D_MODEL), _const_spec((1, D_MODEL)), _const_spec((D_MODEL, IN_COLS)), _const_spec((1, 128)),
        _const_spec((1, Q_LORA)), _const_spec((1, KV_LORA)), _const_spec((Q_LORA, 1024)),
        _const_spec((N_HEADS, 128, KV_LORA)), pl.BlockSpec((bm, TAB_COLS), lambda i: (i % nt, 0)),
    ]
    return pl.pallas_call(
        _in_kernel, grid=(m // bm,), in_specs=in_specs, out_specs=out_specs, out_shape=out_shape,
        compiler_params=_params("parallel"), name="in_proj",
    )(x, W["g_attn"], W["w_in"], W["b_f"], W["g_q"], W["g_kv"], W["w_uq"], W["w_uk"], tab)


def _cumsum_kernel(x_ref, o_ref):
    c = _lane_cumsum(x_ref[...])
    hi = c.astype(BF16).astype(F32)
    r1 = c - hi
    mid = r1.astype(BF16).astype(F32)
    lo = (r1 - mid).astype(BF16).astype(F32)
    pieces = jnp.concatenate([hi, mid, lo, jnp.zeros((128 - 3 * N_HEADS, c.shape[1]), F32)], axis=0)
    o_ref[...] = pieces.T[:, :CP_COLS]


def _cumsum(logft):
    b, _, t = logft.shape
    return pl.pallas_call(
        _cumsum_kernel, grid=(b,),
        in_specs=[pl.BlockSpec((None, N_HEADS, t), lambda i: (i, 0, 0))],
        out_specs=pl.BlockSpec((t, CP_COLS), lambda i: (i, 0)),
        out_shape=jax.ShapeDtypeStruct((b * t, CP_COLS), F32),
        compiler_params=_params("parallel"), name="logf_cumsum",
    )(logft)


def _softmax_step_t(st, pv, m, l, acc):
    m_new = jnp.maximum(m, jnp.max(st, axis=0, keepdims=True))
    alpha = jnp.exp(m - m_new)
    p = jnp.exp(st - m_new)
    l = alpha * l + jnp.sum(p, axis=0, keepdims=True)
    acc = alpha * acc + pv(p)
    return m_new, l, acc


def _causal_mask_t(st, tq):
    key = lax.broadcasted_iota(jnp.int32, st.shape, 0)
    qry = lax.broadcasted_iota(jnp.int32, st.shape, 1) % tq
    return jnp.where(key <= qry, st, NEG_BIG)


N_PAIRS = N_HEADS // 2
MLA_GROUPS = 4


def _fox_kernel(q_ref, k_ref, vt_ref, cp_ref, o_ref, *, tq):
    qi = pl.program_id(1)
    r = 2 * tq
    lane = lax.broadcasted_iota(jnp.int32, (tq, 128), 1)
    xl = lax.broadcasted_iota(jnp.int32, (r, CP_COLS), 1)
    xr = lax.broadcasted_iota(jnp.int32, (r, CP_COLS), 0) // tq
    q2s = []
    for pair in range(N_PAIRS):
        q = q_ref[:, pair * 128:(pair + 1) * 128].astype(F32)
        q2 = jnp.concatenate([jnp.where(lane < HEAD_DIM, q, 0.0), jnp.where(lane >= HEAD_DIM, q, 0.0)], axis=0)
        qx = jnp.where((xl % N_HEADS == 2 * pair + xr) & (xl < 3 * N_HEADS), -1.0, 0.0)
        pad = jnp.zeros((r, 128 - CP_COLS), F32)
        q2s.append(jnp.concatenate([q2, qx, pad], axis=1).T)

    def block(j, carry, diagonal):
        off = pl.multiple_of(j * tq, tq)
        cpb = cp_ref[pl.ds(off, tq), :]
        kpad = jnp.zeros((tq, 128 - CP_COLS), F32)
        sts = []
        for pair in range(N_PAIRS):
            kaug = jnp.concatenate([k_ref[pl.ds(off, tq), pair * 128:(pair + 1) * 128], cpb, kpad], axis=1)
            st = jnp.dot(kaug, q2s[pair], preferred_element_type=F32)
            sts.append(_causal_mask_t(st, tq) if diagonal else st)
        new = []
        for pair in range(N_PAIRS):
            rows = slice(pair * 128, (pair + 1) * 128)
            pv = lambda p, rows=rows: jnp.dot(vt_ref[rows, pl.ds(off, tq)], p, preferred_element_type=F32)
            new.append(_softmax_step_t(sts[pair], pv, *carry[pair]))
        return tuple(new)

    init = tuple((jnp.full((1, r), NEG_BIG, F32), jnp.zeros((1, r), F32), jnp.zeros((128, r), F32))
                 for _ in range(N_PAIRS))
    carry = lax.fori_loop(0, qi, lambda j, c: block(j, c, False), init)
    carry = block(qi, carry, True)
    sub = lax.broadcasted_iota(jnp.int32, (128, tq), 0)
    for pair in range(N_PAIRS):
        m, l, acc = carry[pair]
        ot = acc / l
        o_ref[:, pair * 128:(pair + 1) * 128] = jnp.where(sub < HEAD_DIM, ot[:, :tq], ot[:, tq:]).T


def _fox_attention(fq, fk, fvt, cp, tq):
    b, _, t = fvt.shape
    nq = t // tq
    return pl.pallas_call(
        functools.partial(_fox_kernel, tq=tq), grid=(b, nq),
        in_specs=[pl.BlockSpec((tq, FOX_WIDTH), lambda bi, qi: (bi * nq + qi, 0)),
                  pl.BlockSpec((t, FOX_WIDTH), lambda bi, qi: (bi, 0)),
                  pl.BlockSpec((None, FOX_WIDTH, t), lambda bi, qi: (bi, 0, 0)),
                  pl.BlockSpec((t, CP_COLS), lambda bi, qi: (bi, 0))],
        out_specs=pl.BlockSpec((tq, FOX_WIDTH), lambda bi, qi: (bi * nq + qi, 0)),
        out_shape=jax.ShapeDtypeStruct((b * t, FOX_WIDTH), F32),
        compiler_params=_params("parallel", "arbitrary"), name="fox_attention",
    )(fq, fk, fvt, cp)


def _mla_kernel(q_ref, latr_ref, latt_ref, wuvt_ref, o_ref, *, tq):
    qi = pl.program_id(1)
    hg = N_HEADS // MLA_GROUPS
    r = hg * tq
    qs = []
    for g in range(MLA_GROUPS):
        q = q_ref[g * hg:(g + 1) * hg].reshape(r, LATENT).astype(F32)
        q_rope = jnp.concatenate([q[:, KV_LORA:], jnp.zeros((r, 128 - ROPE), F32)], axis=1)
        qs.append(jnp.concatenate([q[:, :KV_LORA].T, q_rope.T[:ROPE]], axis=0).astype(BF16))

    def block(j, carry, diagonal):
        off = pl.multiple_of(j * tq, tq)
        sts = []
        for g in range(MLA_GROUPS):
            st = jnp.dot(latr_ref[pl.ds(off, tq), :], qs[g], preferred_element_type=F32)
            sts.append(_causal_mask_t(st, tq) if diagonal else st)
        pv = lambda p: jnp.dot(latt_ref[:KV_LORA, pl.ds(off, tq)], p.astype(BF16), preferred_element_type=F32)
        return tuple(_softmax_step_t(sts[g], pv, *carry[g]) for g in range(MLA_GROUPS))

    init = tuple((jnp.full((1, r), NEG_BIG, F32), jnp.zeros((1, r), F32), jnp.zeros((KV_LORA, r), F32))
                 for _ in range(MLA_GROUPS))
    carry = lax.fori_loop(0, qi, lambda j, c: block(j, c, False), init)
    carry = block(qi, carry, True)
    mo_t = []
    for g in range(MLA_GROUPS):
        m, l, acc = carry[g]
        ot = (acc / l).astype(BF16)
        for i in range(hg):
            mo_t.append(jnp.dot(wuvt_ref[g * hg + i], ot[:, i * tq:(i + 1) * tq], preferred_element_type=F32))
    o_ref[...] = jnp.concatenate(mo_t, axis=0).T


def _mla_attention(qlat, latr, lattb, wuvt, tq):
    b, _, t = lattb.shape
    nq = t // tq
    return pl.pallas_call(
        functools.partial(_mla_kernel, tq=tq), grid=(b, nq),
        in_specs=[pl.BlockSpec((N_HEADS, tq, LATENT), lambda bi, qi: (0, bi * nq + qi, 0)),
                  pl.BlockSpec((t, LATENT), lambda bi, qi: (bi, 0)),
                  pl.BlockSpec((None, LATENT, t), lambda bi, qi: (bi, 0, 0)),
                  _const_spec((N_HEADS, HEAD_DIM, KV_LORA))],
        out_specs=pl.BlockSpec((tq, FOX_WIDTH), lambda bi, qi: (bi * nq + qi, 0)),
        out_shape=jax.ShapeDtypeStruct((b * t, FOX_WIDTH), F32),
        compiler_params=_params("parallel", "arbitrary"), name="mla_attention",
    )(qlat, latr, lattb, wuvt)


def _softmax_step(s, pv, m, l, acc):
    m_new = jnp.maximum(m, jnp.max(s, axis=1, keepdims=True))
    alpha = jnp.exp(m - m_new)
    p = jnp.exp(s - m_new)
    l = alpha * l + jnp.sum(p, axis=1, keepdims=True)
    acc = alpha * acc + pv(p)
    return m_new, l, acc


def _decode_kernel(pt_ref, fq_ref, ql_ref, kn_ref, vn_ref, ln_ref, lfn_ref, wuv_ref, *rest, pages):
    kt_refs = rest[0 * pages:1 * pages]
    vt_refs = rest[1 * pages:2 * pages]
    lt_refs = rest[2 * pages:3 * pages]
    ft_refs = rest[3 * pages:4 * pages]
    fo_ref, mo_ref = rest[4 * pages:4 * pages + 2]
    mf, lf, af, mm, lm, am, coff = rest[4 * pages + 2:]
    c = pl.program_id(1)
    nc = pl.num_programs(1)

    @pl.when(c == 0)
    def _():
        mf[...] = jnp.full(mf.shape, NEG_BIG, F32)
        mm[...] = jnp.full(mm.shape, NEG_BIG, F32)
        for ref in (lf, af, lm, am, coff):
            ref[...] = jnp.zeros(ref.shape, F32)

    hrow = lax.broadcasted_iota(jnp.int32, (N_HEADS, FOX_WIDTH), 0)
    hcol = lax.broadcasted_iota(jnp.int32, (N_HEADS, FOX_WIDTH), 1) // HEAD_DIM
    own = hrow == hcol
    qbd = jnp.where(own, fq_ref[0].astype(F32), 0.0)
    qlat = ql_ref[0].astype(F32)
    cat = lambda refs: jnp.concatenate([r_[0] for r_ in refs], axis=1)

    cum = _lane_cumsum(cat(ft_refs)) + coff[:, 0:1]
    tk = cum.shape[1]
    c_last = cum[:, tk - 1:tk]
    coff[:, 0:1] = c_last

    kt = cat(kt_refs)
    vt = cat(vt_refs)
    s = jnp.dot(qbd, kt, preferred_element_type=F32) - cum
    pv = lambda p: lax.dot_general(p, vt, NT, preferred_element_type=F32)
    m_new, l_new, a_new = _softmax_step(s, pv, mf[:, 0:1], lf[:, 0:1], af[...])

    lt = cat(lt_refs)
    s2 = jnp.dot(qlat, lt, preferred_element_type=F32)
    pv2 = lambda p: lax.dot_general(p, lt[:KV_LORA], NT, preferred_element_type=F32)
    m2, l2, a2 = _softmax_step(s2, pv2, mm[:, 0:1], lm[:, 0:1], am[...])

    @pl.when(c < nc - 1)
    def _():
        mf[:, 0:1] = m_new
        lf[:, 0:1] = l_new
        af[...] = a_new
        mm[:, 0:1] = m2
        lm[:, 0:1] = l2
        am[...] = a2

    @pl.when(c == nc - 1)
    def _():
        c_tot = c_last + lfn_ref[0]
        sn = jnp.sum(qbd * kn_ref[0], axis=1, keepdims=True) - c_tot
        mfin = jnp.maximum(m_new, sn)
        al = jnp.exp(m_new - mfin)
        pn = jnp.exp(sn - mfin)
        o = (al * a_new + pn * vn_ref[0]) / (al * l_new + pn)
        fo_ref[0] = jnp.sum(jnp.where(own, o, 0.0), axis=0, keepdims=True)

        ln = ln_ref[0]
        sn2 = jnp.sum(qlat * ln, axis=1, keepdims=True)
        mfin2 = jnp.maximum(m2, sn2)
        al2 = jnp.exp(m2 - mfin2)
        pn2 = jnp.exp(sn2 - mfin2)
        o2 = ((al2 * a2 + pn2 * ln[:, :KV_LORA]) / (al2 * l2 + pn2)).astype(BF16)
        mo = jnp.dot(o2, wuv_ref[...], preferred_element_type=F32)
        mo_ref[0] = jnp.sum(jnp.where(own, mo, 0.0), axis=0, keepdims=True)


def _decode_attention(page_table, fq, qlat_t, fk_new, fv_new, lat_new, lf_new, wuv, ckt, cvt, clt, cft, pages):
    nb, n_pages = page_table.shape
    nc = n_pages // pages
    pt = page_table.reshape(-1)

    def page_spec(rows, i):
        return pl.BlockSpec((1, rows, PAGE), lambda b, c, pt_: (pt_[b * n_pages + c * pages + i], 0, 0))

    seq = lambda *shape: pl.BlockSpec((1,) + shape, lambda b, c, pt_: (b,) + (0,) * len(shape))
    in_specs = [seq(1, FOX_WIDTH), seq(N_HEADS, LATENT), seq(1, FOX_WIDTH), seq(1, FOX_WIDTH),
                seq(1, LATENT), seq(N_HEADS, 1), _const_spec((KV_LORA, FOX_WIDTH))]
    operands = [fq.reshape(nb, 1, FOX_WIDTH), qlat_t, fk_new.reshape(nb, 1, FOX_WIDTH),
                fv_new.reshape(nb, 1, FOX_WIDTH), lat_new.reshape(nb, 1, LATENT), lf_new.reshape(nb, N_HEADS, 1), wuv]
    for arr, rows in ((ckt, FOX_WIDTH), (cvt, FOX_WIDTH), (clt, LATENT), (cft, N_HEADS)):
        for i in range(pages):
            in_specs.append(page_spec(rows, i))
            operands.append(arr)
    scratch = [pltpu.VMEM((N_HEADS, 128), F32), pltpu.VMEM((N_HEADS, 128), F32), pltpu.VMEM((N_HEADS, FOX_WIDTH), F32),
               pltpu.VMEM((N_HEADS, 128), F32), pltpu.VMEM((N_HEADS, 128), F32), pltpu.VMEM((N_HEADS, KV_LORA), F32),
               pltpu.VMEM((N_HEADS, 128), F32)]
    grid_spec = pltpu.PrefetchScalarGridSpec(
        num_scalar_prefetch=1, grid=(nb, nc), in_specs=in_specs,
        out_specs=(seq(1, FOX_WIDTH), seq(1, FOX_WIDTH)),
        scratch_shapes=scratch)
    return pl.pallas_call(
        functools.partial(_decode_kernel, pages=pages), grid_spec=grid_spec,
        out_shape=(jax.ShapeDtypeStruct((nb, 1, FOX_WIDTH), F32), jax.ShapeDtypeStruct((nb, 1, FOX_WIDTH), F32)),
        compiler_params=_params("parallel", "arbitrary"), name="decode_attention",
    )(pt, *operands)


def _post_kernel(x_ref, fo_ref, mo_ref, gfo_ref, gmo_ref, wo_ref, h_ref):
    fo = _rms(fo_ref[...], gfo_ref[...]).astype(BF16)
    mo = _rms(mo_ref[...], gmo_ref[...]).astype(BF16)
    o = jnp.dot(fo, wo_ref[:FOX_WIDTH, :], preferred_element_type=F32)
    o = o + jnp.dot(mo, wo_ref[FOX_WIDTH:, :], preferred_element_type=F32)
    h_ref[...] = x_ref[...] + o


def _post(x, fox_o, mla_o, W, bm):
    m = x.shape[0]
    row = lambda w_: pl.BlockSpec((bm, w_), lambda i: (i, 0))
    return pl.pallas_call(
        _post_kernel, grid=(m // bm,),
        in_specs=[row(D_MODEL), row(FOX_WIDTH), row(FOX_WIDTH), _const_spec((1, FOX_WIDTH)), _const_spec((1, FOX_WIDTH)),
                  _const_spec((D_MODEL, D_MODEL))],
        out_specs=row(D_MODEL), out_shape=jax.ShapeDtypeStruct((m, D_MODEL), F32),
        compiler_params=_params("parallel"), name="attn_out_proj",
    )(x, fox_o, mla_o, W["g_fox_out"], W["g_mla_out"], W["w_o"])


def _ffn_kernel(h_ref, p_ref, gffn_ref, wg_ref, wu_ref, wd_ref, gple_ref, wpg_ref, wpp_ref, gfin_ref, y_ref):
    h = h_ref[...]
    hn = _rms(h, gffn_ref[...]).astype(BF16)
    g = jnp.dot(hn, wg_ref[...], preferred_element_type=F32)
    u = jnp.dot(hn, wu_ref[...], preferred_element_type=F32)
    act = (g * (1.0 / (1.0 + jnp.exp(-g))) * u).astype(BF16)
    h = h + jnp.dot(act, wd_ref[...], preferred_element_type=F32)

    hp = _rms(h, gple_ref[...]).astype(BF16)
    zg = jnp.dot(hp, wpg_ref[...], preferred_element_type=F32)
    gate = 1.0 / (1.0 + jnp.exp(-zg))
    proj = jnp.dot(p_ref[...].astype(BF16), wpp_ref[...], preferred_element_type=F32)
    h = h + gate * proj
    y_ref[...] = _rms(h, gfin_ref[...])


def _ffn(h, p, W, bm):
    m = h.shape[0]
    row = lambda w_: pl.BlockSpec((bm, w_), lambda i: (i, 0))
    return pl.pallas_call(
        _ffn_kernel, grid=(m // bm,),
        in_specs=[row(D_MODEL), row(PLE_DIM), _const_spec((1, D_MODEL)), _const_spec((D_MODEL, D_FF)),
                  _const_spec((D_MODEL, D_FF)), _const_spec((D_FF, D_MODEL)), _const_spec((1, D_MODEL)),
                  _const_spec((D_MODEL, D_MODEL)), _const_spec((PLE_DIM, D_MODEL)), _const_spec((1, D_MODEL))],
        out_specs=row(D_MODEL), out_shape=jax.ShapeDtypeStruct((m, D_MODEL), F32),
        compiler_params=_params("parallel"), name="ffn_ple_norm",
    )(h, p, W["g_ffn"], W["w_gate"], W["w_up"], W["w_down"], W["g_ple"], W["w_ple_gate"], W["w_ple_proj"],
      W["g_final"])


def _prep_weights(g_attn, w_in, b_f, g_q, w_uq, g_kv, w_uk, w_uv, g_fox_out, g_mla_out, w_o, g_ffn, w_gate, w_up,
                  w_down, g_ple, w_ple_gate, w_ple_proj, g_final):
    w = w_in[0]
    o = np.cumsum((0, 512, 512, 512, N_HEADS, Q_LORA, KV_LORA, ROPE))
    fq, fk, fv, fl, qc, kv, kr = (w[:, o[i]:o[i + 1]] for i in range(7))
    kr_sw = jnp.concatenate([kr[:, ROPE // 2:], kr[:, :ROPE // 2]], axis=1)
    misc = jnp.concatenate([kr, kr_sw, fl, jnp.zeros((D_MODEL, 128 - 2 * ROPE - N_HEADS), F32)], axis=1)
    w_in_p = jnp.concatenate([fq, fk, fv, qc, kv, misc], axis=1).astype(BF16)

    uq = w_uq[0].reshape(Q_LORA, N_HEADS, NOPE + ROPE)
    uq_nope = uq[:, :, :NOPE].reshape(Q_LORA, N_HEADS * NOPE)
    uq_rope = uq[:, :, NOPE:]
    uq_rope_sw = jnp.concatenate([uq_rope[:, :, ROPE // 2:], uq_rope[:, :, :ROPE // 2]], axis=2)
    w_uq_p = jnp.concatenate([uq_nope, uq_rope.reshape(Q_LORA, -1), uq_rope_sw.reshape(Q_LORA, -1)], axis=1).astype(BF16)

    uk = jnp.transpose(w_uk[0], (1, 2, 0))
    zeros = jnp.zeros_like(uk)
    even = jnp.concatenate([uk, zeros], axis=1)
    odd = jnp.concatenate([zeros, uk], axis=1)
    w_uk_p = jnp.where((jnp.arange(N_HEADS) % 2 == 0)[:, None, None], even, odd).astype(BF16)

    b_row = jnp.zeros((1, 128), F32).at[0, M_FL:M_FL + N_HEADS].set(b_f[0])
    return dict(
        g_attn=g_attn, w_in=w_in_p, b_f=b_row, g_q=g_q, g_kv=g_kv, w_uq=w_uq_p, w_uk=w_uk_p,
        w_uv_t=jnp.transpose(w_uv[0], (1, 2, 0)).astype(BF16),
        w_uv_flat=w_uv[0].reshape(KV_LORA, FOX_WIDTH).astype(BF16),
        g_fox_out=g_fox_out, g_mla_out=g_mla_out,
        w_o=w_o[0].astype(BF16), g_ffn=g_ffn, w_gate=w_gate[0].astype(BF16), w_up=w_up[0].astype(BF16),
        w_down=w_down[0].astype(BF16), g_ple=g_ple, w_ple_gate=w_ple_gate[0].astype(BF16),
        w_ple_proj=w_ple_proj[0].astype(BF16), g_final=g_final.reshape(1, D_MODEL))


def _rope_tables(pos):
    half = ROPE // 2
    inv = ROPE_THETA ** (-jnp.arange(half, dtype=F32) / half)
    ang = pos.astype(F32)[:, None] * inv[None, :]
    cos, sin = jnp.cos(ang), jnp.sin(ang)
    c32 = jnp.concatenate([cos, cos], axis=1)
    s32 = jnp.concatenate([-sin, sin], axis=1)
    pad = jnp.zeros((pos.shape[0], 128 - 2 * ROPE), F32)
    return jnp.concatenate([jnp.tile(c32, (1, N_HEADS)), jnp.tile(s32, (1, N_HEADS)), c32, s32, pad], axis=1)


def _block(m, target):
    b = min(m, target)
    while m % b:
        b //= 2
    return b


def kernel(x_prompt, x_sample, cache_fox_k, cache_fox_v, cache_fox_logf, cache_mla_latent, page_table, p_prompt, p_sample, g_attn, w_in, b_f, g_q, w_uq, g_kv, w_uk, w_uv, g_fox_out, g_mla_out, w_o, g_ffn, w_gate, w_up, w_down, g_ple, w_ple_gate, w_ple_proj, g_final):
    assert w_in.shape[0] == 1, "one layer"
    W = _prep_weights(g_attn, w_in, b_f, g_q, w_uq, g_kv, w_uk, w_uv, g_fox_out, g_mla_out, w_o, g_ffn, w_gate,
                      w_up, w_down, g_ple, w_ple_gate, w_ple_proj, g_final)
    b, t, _ = x_prompt.shape
    nb, ts, _ = x_sample.shape
    assert ts == 1
    n_pool = cache_fox_k.shape[1]
    past_len = page_table.shape[1] * PAGE

    def heads_last(xt, lead):
        return jnp.transpose(xt.reshape(lead, N_HEADS, HEAD_DIM, -1), (0, 3, 1, 2))

    mp = b * t
    xp = x_prompt.reshape(mp, D_MODEL)
    tq = _block(t, 256)
    tab = _rope_tables(jnp.arange(t, dtype=jnp.int32))
    fq, fk, fkt, fvt, logft, latt, lattb, latr, qlat = _in_proj(xp, W, tab, b, t, tq)
    cp = _cumsum(logft)
    fox_o = _fox_attention(fq, fk, fvt, cp, tq)
    mla_o = _mla_attention(qlat, latr, lattb, W["w_uv_t"], tq)
    h1 = _post(xp, fox_o, mla_o, W, _block(mp, 256))
    y_p = _ffn(h1, p_prompt.reshape(mp, PLE_DIM), W, _block(mp, 256))

    xs = x_sample.reshape(nb, D_MODEL)
    tab_s = _rope_tables(jnp.full((nb,), past_len, jnp.int32))
    fq_s, fk_s, fkt_s, fvt_s, logft_s, latt_s, _, _, qlat_s = _in_proj(xs, W, tab_s, 1, nb, nb)
    fox_s, mla_s = _decode_attention(
        page_table, fq_s, jnp.transpose(qlat_s, (1, 0, 2)), fk_s, fvt_s[0].T, latt_s[0].T, logft_s[0].T,
        W["w_uv_flat"],
        jnp.transpose(cache_fox_k[0], (0, 2, 3, 1)).reshape(n_pool, FOX_WIDTH, PAGE),
        jnp.transpose(cache_fox_v[0], (0, 2, 3, 1)).reshape(n_pool, FOX_WIDTH, PAGE),
        jnp.transpose(cache_mla_latent[0], (0, 2, 1)), jnp.transpose(cache_fox_logf[0], (0, 2, 1)),
        pages=_block(page_table.shape[1], 16))
    h1s = _post(xs, fox_s.reshape(nb, FOX_WIDTH), mla_s.reshape(nb, FOX_WIDTH), W, nb)
    y_s = _ffn(h1s, p_sample.reshape(nb, PLE_DIM), W, nb)

    return (
        y_p.reshape(b, t, D_MODEL), y_s.reshape(nb, 1, D_MODEL),
        heads_last(fkt, b)[None], heads_last(fvt, b)[None],
        jnp.transpose(logft, (0, 2, 1))[None], jnp.transpose(latt, (0, 2, 1))[None],
        heads_last(fkt_s, 1).reshape(1, nb, 1, N_HEADS, HEAD_DIM), heads_last(fvt_s, 1).reshape(1, nb, 1, N_HEADS, HEAD_DIM),
        jnp.transpose(logft_s, (0, 2, 1)).reshape(1, nb, 1, N_HEADS),
        jnp.transpose(latt_s, (0, 2, 1)).reshape(1, nb, 1, LATENT),
    )
```

```python
import functools

import jax
import jax.numpy as jnp
import numpy as np
from jax import lax
from jax.experimental import pallas as pl
from jax.experimental.pallas import tpu as pltpu

D_MODEL = 1024
HEAD_DIM = 64
N_HEADS = 8
FOX_WIDTH = N_HEADS * HEAD_DIM
NOPE = 64
ROPE = 32
Q_LORA = 384
KV_LORA = 256
LATENT = KV_LORA + ROPE
D_FF = 2816
PLE_DIM = 256
PAGE = 128
ROPE_THETA = 10000.0
RMS_EPS = 1e-6
LOG2E = 1.4426950408889634
FOX_SCALE = HEAD_DIM ** -0.5 * LOG2E
MLA_SCALE = (NOPE + ROPE) ** -0.5 * LOG2E
NEG_BIG = -1e30

C_FQ, C_FK, C_FV, C_QC, C_KV, C_MISC = 0, 512, 1024, 1536, 1920, 2176
IN_COLS = 2304
M_KR, M_KRSW, M_FL = 0, 32, 64
TAB_COLS = 640
CP_COLS = 32

F32 = jnp.float32
BF16 = jnp.bfloat16
VMEM_LIMIT = 56 * 1024 * 1024

NT = (((1,), (1,)), ((), ()))


def _rms(x, g):
    return x * lax.rsqrt(jnp.mean(x * x, axis=-1, keepdims=True) + RMS_EPS) * g


def _const_spec(shape):
    nd = len(shape)
    return pl.BlockSpec(shape, lambda *_: (0,) * nd, pipeline_mode=pl.Buffered(1))


def _params(*sem):
    return pltpu.CompilerParams(dimension_semantics=sem, vmem_limit_bytes=VMEM_LIMIT)


def _lane_cumsum(x):
    n = x.shape[-1]
    lane = lax.broadcasted_iota(jnp.int32, x.shape, x.ndim - 1)
    sh = 1
    while sh < n:
        x = x + jnp.where(lane >= sh, pltpu.roll(x, sh, axis=x.ndim - 1), 0.0)
        sh *= 2
    return x


def _in_kernel(x_ref, gat_ref, win_ref, bf_ref, gq_ref, gkv_ref, wuq_ref, wuk_ref, tab_ref,
               fq_ref, fk_ref, fkt_ref, fvt_ref, logft_ref, latt_ref, lattb_ref, latr_ref, qlat_ref):
    x = x_ref[...]
    hn = _rms(x, gat_ref[...]).astype(BF16)
    z = jnp.dot(hn, win_ref[...], preferred_element_type=F32)

    fq_ref[...] = (z[:, C_FQ:C_FQ + FOX_WIDTH] * FOX_SCALE).astype(BF16)
    fk_ref[...] = z[:, C_FK:C_FK + FOX_WIDTH]
    fkt_ref[...] = z[:, C_FK:C_FK + FOX_WIDTH].T
    fvt_ref[...] = z[:, C_FV:C_FV + FOX_WIDTH].T

    misc = z[:, C_MISC:C_MISC + 128]
    t = misc + bf_ref[...]
    lf = jnp.minimum(t, 0.0) - jnp.log1p(jnp.exp(-jnp.abs(t)))
    lane = lax.broadcasted_iota(jnp.int32, misc.shape, 1)
    rot = misc * tab_ref[:, 512:640]
    mt = jnp.where(lane < M_FL, rot, lf).T
    k_rot_t = mt[M_KR:M_KR + ROPE] + mt[M_KRSW:M_KRSW + ROPE]
    logft_ref[...] = mt[M_FL:M_FL + N_HEADS]

    kvn = _rms(z[:, C_KV:C_KV + KV_LORA], gkv_ref[...])
    kvn_t = kvn.T
    latt_ref[:KV_LORA, :] = kvn_t
    latt_ref[KV_LORA:, :] = k_rot_t
    lattb_ref[:KV_LORA, :] = kvn_t.astype(BF16)
    lattb_ref[KV_LORA:, :] = k_rot_t.astype(BF16)
    latr_ref[:, :KV_LORA] = kvn.astype(BF16)
    latr_ref[:, KV_LORA:] = (rot[:, M_KR:M_KR + ROPE] + rot[:, M_KRSW:M_KRSW + ROPE]).astype(BF16)

    qn = _rms(z[:, C_QC:C_QC + Q_LORA], gq_ref[...]).astype(BF16)
    q = jnp.dot(qn, wuq_ref[...], preferred_element_type=F32)
    q_rot = q[:, 512:768] * tab_ref[:, 0:256] + q[:, 768:1024] * tab_ref[:, 256:512]
    q_rot = (q_rot * MLA_SCALE).astype(BF16)
    for h in range(N_HEADS):
        p = h // 2
        q_pair = q[:, p * 128:(p + 1) * 128].astype(BF16)
        ql = jnp.dot(q_pair, wuk_ref[h], preferred_element_type=F32)
        qlat_ref[h, :, :KV_LORA] = (ql * MLA_SCALE).astype(BF16)
        qlat_ref[h, :, KV_LORA:] = q_rot[:, h * ROPE:(h + 1) * ROPE]


def _in_proj(x, W, tab, b, t, bm):
    m = b * t
    nt = t // bm
    row = lambda w: pl.BlockSpec((bm, w), lambda i: (i, 0))
    tr = lambda w: pl.BlockSpec((None, w, bm), lambda i: (i // nt, 0, i % nt))
    out_shape = (
        jax.ShapeDtypeStruct((m, FOX_WIDTH), BF16),
        jax.ShapeDtypeStruct((m, FOX_WIDTH), F32),
        jax.ShapeDtypeStruct((b, FOX_WIDTH, t), F32),
        jax.ShapeDtypeStruct((b, FOX_WIDTH, t), F32),
        jax.ShapeDtypeStruct((b, N_HEADS, t), F32),
        jax.ShapeDtypeStruct((b, LATENT, t), F32),
        jax.ShapeDtypeStruct((b, LATENT, t), BF16),
        jax.ShapeDtypeStruct((m, LATENT), BF16),
        jax.ShapeDtypeStruct((N_HEADS, m, LATENT), BF16),
    )
    out_specs = (row(FOX_WIDTH), row(FOX_WIDTH), tr(FOX_WIDTH), tr(FOX_WIDTH), tr(N_HEADS), tr(LATENT), tr(LATENT),
                 row(LATENT), pl.BlockSpec((N_HEADS, bm, LATENT), lambda i: (0, i, 0)))
    in_specs = [
        row(D_MODEL), _const_spec((1, D_MODEL)), _const_spec((D_MODEL, IN_COLS)), _const_spec((1, 128)),
        _const_spec((1, Q_LORA)), _const_spec((1, KV_LORA)), _const_spec((Q_LORA, 1024)),
        _const_spec((N_HEADS, 128, KV_LORA)), pl.BlockSpec((bm, TAB_COLS), lambda i: (i % nt, 0)),
    ]
    return pl.pallas_call(
        _in_kernel, grid=(m // bm,), in_specs=in_specs, out_specs=out_specs, out_shape=out_shape,
        compiler_params=_params("parallel"), name="in_proj",
    )(x, W["g_attn"], W["w_in"], W["b_f"], W["g_q"], W["g_kv"], W["w_uq"], W["w_uk"], tab)


def _cumsum_kernel(x_ref, o_ref):
    c = _lane_cumsum(x_ref[...]) * LOG2E
    hi = c.astype(BF16).astype(F32)
    r1 = c - hi
    mid = r1.astype(BF16).astype(F32)
    lo = (r1 - mid).astype(BF16).astype(F32)
    pieces = jnp.concatenate([hi, mid, lo, jnp.zeros((128 - 3 * N_HEADS, c.shape[1]), F32)], axis=0)
    o_ref[...] = pieces.T[:, :CP_COLS]


def _cumsum(logft):
    b, _, t = logft.shape
    return pl.pallas_call(
        _cumsum_kernel, grid=(b,),
        in_specs=[pl.BlockSpec((None, N_HEADS, t), lambda i: (i, 0, 0))],
        out_specs=pl.BlockSpec((t, CP_COLS), lambda i: (i, 0)),
        out_shape=jax.ShapeDtypeStruct((b * t, CP_COLS), F32),
        compiler_params=_params("parallel"), name="logf_cumsum",
    )(logft)


def _softmax_step_t(st, pv, m, l, acc):
    m_new = jnp.maximum(m, jnp.max(st, axis=0, keepdims=True))
    alpha = jnp.exp2(m - m_new)
    p = jnp.exp2(st - m_new)
    l = alpha * l + jnp.sum(p, axis=0, keepdims=True)
    acc = alpha * acc + pv(p)
    return m_new, l, acc


def _causal_mask_t(st, tq):
    key = lax.broadcasted_iota(jnp.int32, st.shape, 0)
    qry = lax.broadcasted_iota(jnp.int32, st.shape, 1) % tq
    return jnp.where(key <= qry, st, NEG_BIG)


N_PAIRS = N_HEADS // 2
MLA_GROUPS = 4


def _fox_kernel(q_ref, k_ref, vt_ref, cp_ref, o_ref, *, tq):
    qi = pl.program_id(1)
    r = 2 * tq
    lane = lax.broadcasted_iota(jnp.int32, (tq, 128), 1)
    xl = lax.broadcasted_iota(jnp.int32, (r, CP_COLS), 1)
    xr = lax.broadcasted_iota(jnp.int32, (r, CP_COLS), 0) // tq
    q2s = []
    for pair in range(N_PAIRS):
        q = q_ref[:, pair * 128:(pair + 1) * 128].astype(F32)
        q2 = jnp.concatenate([jnp.where(lane < HEAD_DIM, q, 0.0), jnp.where(lane >= HEAD_DIM, q, 0.0)], axis=0)
        qx = jnp.where((xl % N_HEADS == 2 * pair + xr) & (xl < 3 * N_HEADS), -1.0, 0.0)
        pad = jnp.zeros((r, 128 - CP_COLS), F32)
        q2s.append(jnp.concatenate([q2, qx, pad], axis=1).T)

    def block(j, carry, diagonal):
        off = pl.multiple_of(j * tq, tq)
        cpb = cp_ref[pl.ds(off, tq), :]
        kpad = jnp.zeros((tq, 128 - CP_COLS), F32)
        sts = []
        for pair in range(N_PAIRS):
            kaug = jnp.concatenate([k_ref[pl.ds(off, tq), pair * 128:(pair + 1) * 128], cpb, kpad], axis=1)
            st = jnp.dot(kaug, q2s[pair], preferred_element_type=F32)
            sts.append(_causal_mask_t(st, tq) if diagonal else st)
        new = []
        for pair in range(N_PAIRS):
            rows = slice(pair * 128, (pair + 1) * 128)
            pv = lambda p, rows=rows: jnp.dot(vt_ref[rows, pl.ds(off, tq)], p, preferred_element_type=F32)
            new.append(_softmax_step_t(sts[pair], pv, *carry[pair]))
        return tuple(new)

    init = tuple((jnp.full((1, r), NEG_BIG, F32), jnp.zeros((1, r), F32), jnp.zeros((128, r), F32))
                 for _ in range(N_PAIRS))
    carry = lax.fori_loop(0, qi, lambda j, c: block(j, c, False), init)
    carry = block(qi, carry, True)
    sub = lax.broadcasted_iota(jnp.int32, (128, tq), 0)
    for pair in range(N_PAIRS):
        m, l, acc = carry[pair]
        ot = acc / l
        o_ref[:, pair * 128:(pair + 1) * 128] = jnp.where(sub < HEAD_DIM, ot[:, :tq], ot[:, tq:]).T


def _fox_attention(fq, fk, fvt, cp, tq):
    b, _, t = fvt.shape
    nq = t // tq
    return pl.pallas_call(
        functools.partial(_fox_kernel, tq=tq), grid=(b, nq),
        in_specs=[pl.BlockSpec((tq, FOX_WIDTH), lambda bi, qi: (bi * nq + qi, 0)),
                  pl.BlockSpec((t, FOX_WIDTH), lambda bi, qi: (bi, 0)),
                  pl.BlockSpec((None, FOX_WIDTH, t), lambda bi, qi: (bi, 0, 0)),
                  pl.BlockSpec((t, CP_COLS), lambda bi, qi: (bi, 0))],
        out_specs=pl.BlockSpec((tq, FOX_WIDTH), lambda bi, qi: (bi * nq + qi, 0)),
        out_shape=jax.ShapeDtypeStruct((b * t, FOX_WIDTH), F32),
        compiler_params=_params("parallel", "arbitrary"), name="fox_attention",
    )(fq, fk, fvt, cp)


def _mla_kernel(q_ref, latr_ref, latt_ref, wuvt_ref, o_ref, *, tq):
    qi = pl.program_id(1)
    hg = N_HEADS // MLA_GROUPS
    r = hg * tq
    qs = []
    for g in range(MLA_GROUPS):
        q = q_ref[g * hg:(g + 1) * hg].reshape(r, LATENT).astype(F32)
        q_rope = jnp.concatenate([q[:, KV_LORA:], jnp.zeros((r, 128 - ROPE), F32)], axis=1)
        qs.append(jnp.concatenate([q[:, :KV_LORA].T, q_rope.T[:ROPE]], axis=0).astype(BF16))

    def block(j, carry, diagonal):
        off = pl.multiple_of(j * tq, tq)
        sts = []
        for g in range(MLA_GROUPS):
            st = jnp.dot(latr_ref[pl.ds(off, tq), :], qs[g], preferred_element_type=F32)
            sts.append(_causal_mask_t(st, tq) if diagonal else st)
        pv = lambda p: jnp.dot(latt_ref[:KV_LORA, pl.ds(off, tq)], p.astype(BF16), preferred_element_type=F32)
        return tuple(_softmax_step_t(sts[g], pv, *carry[g]) for g in range(MLA_GROUPS))

    init = tuple((jnp.full((1, r), NEG_BIG, F32), jnp.zeros((1, r), F32), jnp.zeros((KV_LORA, r), F32))
                 for _ in range(MLA_GROUPS))
    carry = lax.fori_loop(0, qi, lambda j, c: block(j, c, False), init)
    carry = block(qi, carry, True)
    mo_t = []
    for g in range(MLA_GROUPS):
        m, l, acc = carry[g]
        ot = (acc / l).astype(BF16)
        for i in range(hg):
            mo_t.append(jnp.dot(wuvt_ref[g * hg + i], ot[:, i * tq:(i + 1) * tq], preferred_element_type=F32))
    o_ref[...] = jnp.concatenate(mo_t, axis=0).T


def _mla_attention(qlat, latr, lattb, wuvt, tq):
    b, _, t = lattb.shape
    nq = t // tq
    return pl.pallas_call(
        functools.partial(_mla_kernel, tq=tq), grid=(b, nq),
        in_specs=[pl.BlockSpec((N_HEADS, tq, LATENT), lambda bi, qi: (0, bi * nq + qi, 0)),
                  pl.BlockSpec((t, LATENT), lambda bi, qi: (bi, 0)),
                  pl.BlockSpec((None, LATENT, t), lambda bi, qi: (bi, 0, 0)),
                  _const_spec((N_HEADS, HEAD_DIM, KV_LORA))],
        out_specs=pl.BlockSpec((tq, FOX_WIDTH), lambda bi, qi: (bi * nq + qi, 0)),
        out_shape=jax.ShapeDtypeStruct((b * t, FOX_WIDTH), F32),
        compiler_params=_params("parallel", "arbitrary"), name="mla_attention",
    )(qlat, latr, lattb, wuvt)


def _softmax_step(s, pv, m, l, acc):
    m_new = jnp.maximum(m, jnp.max(s, axis=1, keepdims=True))
    alpha = jnp.exp2(m - m_new)
    p = jnp.exp2(s - m_new)
    l = alpha * l + jnp.sum(p, axis=1, keepdims=True)
    acc = alpha * acc + pv(p)
    return m_new, l, acc


def _decode_kernel(pt_ref, fq_ref, ql_ref, kn_ref, vn_ref, ln_ref, lfn_ref, wuv_ref, *rest, pages):
    kt_refs = rest[0 * pages:1 * pages]
    vt_refs = rest[1 * pages:2 * pages]
    lt_refs = rest[2 * pages:3 * pages]
    ft_refs = rest[3 * pages:4 * pages]
    fo_ref, mo_ref = rest[4 * pages:4 * pages + 2]
    mf, lf, af, mm, lm, am, coff = rest[4 * pages + 2:]
    c = pl.program_id(1)
    nc = pl.num_programs(1)

    @pl.when(c == 0)
    def _():
        mf[...] = jnp.full(mf.shape, NEG_BIG, F32)
        mm[...] = jnp.full(mm.shape, NEG_BIG, F32)
        for ref in (lf, af, lm, am, coff):
            ref[...] = jnp.zeros(ref.shape, F32)

    hrow = lax.broadcasted_iota(jnp.int32, (N_HEADS, FOX_WIDTH), 0)
    hcol = lax.broadcasted_iota(jnp.int32, (N_HEADS, FOX_WIDTH), 1) // HEAD_DIM
    own = hrow == hcol
    qbd = jnp.where(own, fq_ref[0].astype(F32), 0.0)
    qlat = ql_ref[0].astype(F32)
    cat = lambda refs: jnp.concatenate([r_[0] for r_ in refs], axis=1)

    kt = cat(kt_refs)
    lt = cat(lt_refs)
    s = jnp.dot(qbd, kt, preferred_element_type=F32)
    s2 = jnp.dot(qlat, lt, preferred_element_type=F32)

    cum = _lane_cumsum(cat(ft_refs)) + coff[:, 0:1]
    tk = cum.shape[1]
    c_last = cum[:, tk - 1:tk]
    coff[:, 0:1] = c_last

    vt = cat(vt_refs)
    pv = lambda p: lax.dot_general(p, vt, NT, preferred_element_type=F32)
    m_new, l_new, a_new = _softmax_step(s - cum * LOG2E, pv, mf[:, 0:1], lf[:, 0:1], af[...])
    pv2 = lambda p: lax.dot_general(p, lt[:KV_LORA], NT, preferred_element_type=F32)
    m2, l2, a2 = _softmax_step(s2, pv2, mm[:, 0:1], lm[:, 0:1], am[...])

    @pl.when(c < nc - 1)
    def _():
        mf[:, 0:1] = m_new
        lf[:, 0:1] = l_new
        af[...] = a_new
        mm[:, 0:1] = m2
        lm[:, 0:1] = l2
        am[...] = a2

    @pl.when(c == nc - 1)
    def _():
        c_tot = c_last + lfn_ref[0]
        sn = jnp.sum(qbd * kn_ref[0], axis=1, keepdims=True) - c_tot * LOG2E
        mfin = jnp.maximum(m_new, sn)
        al = jnp.exp2(m_new - mfin)
        pn = jnp.exp2(sn - mfin)
        o = (al * a_new + pn * vn_ref[0]) / (al * l_new + pn)
        fo_ref[0] = jnp.sum(jnp.where(own, o, 0.0), axis=0, keepdims=True)

        ln = ln_ref[0]
        sn2 = jnp.sum(qlat * ln, axis=1, keepdims=True)
        mfin2 = jnp.maximum(m2, sn2)
        al2 = jnp.exp2(m2 - mfin2)
        pn2 = jnp.exp2(sn2 - mfin2)
        o2 = ((al2 * a2 + pn2 * ln[:, :KV_LORA]) / (al2 * l2 + pn2)).astype(BF16)
        mo = jnp.dot(o2, wuv_ref[...], preferred_element_type=F32)
        mo_ref[0] = jnp.sum(jnp.where(own, mo, 0.0), axis=0, keepdims=True)


def _decode_attention(page_table, fq, qlat_t, fk_new, fv_new, lat_new, lf_new, wuv, ckt, cvt, clt, cft, pages):
    nb, n_pages = page_table.shape
    nc = n_pages // pages
    pt = page_table.reshape(-1)

    def page_spec(rows, i):
        return pl.BlockSpec((1, rows, PAGE), lambda b, c, pt_: (pt_[b * n_pages + c * pages + i], 0, 0))

    seq = lambda *shape: pl.BlockSpec((1,) + shape, lambda b, c, pt_: (b,) + (0,) * len(shape))
    in_specs = [seq(1, FOX_WIDTH), seq(N_HEADS, LATENT), seq(1, FOX_WIDTH), seq(1, FOX_WIDTH),
                seq(1, LATENT), seq(N_HEADS, 1), _const_spec((KV_LORA, FOX_WIDTH))]
    operands = [fq.reshape(nb, 1, FOX_WIDTH), qlat_t, fk_new.reshape(nb, 1, FOX_WIDTH),
                fv_new.reshape(nb, 1, FOX_WIDTH), lat_new.reshape(nb, 1, LATENT), lf_new.reshape(nb, N_HEADS, 1), wuv]
    for arr, rows in ((ckt, FOX_WIDTH), (cvt, FOX_WIDTH), (clt, LATENT), (cft, N_HEADS)):
        for i in range(pages):
            in_specs.append(page_spec(rows, i))
            operands.append(arr)
    scratch = [pltpu.VMEM((N_HEADS, 128), F32), pltpu.VMEM((N_HEADS, 128), F32), pltpu.VMEM((N_HEADS, FOX_WIDTH), F32),
               pltpu.VMEM((N_HEADS, 128), F32), pltpu.VMEM((N_HEADS, 128), F32), pltpu.VMEM((N_HEADS, KV_LORA), F32),
               pltpu.VMEM((N_HEADS, 128), F32)]
    grid_spec = pltpu.PrefetchScalarGridSpec(
        num_scalar_prefetch=1, grid=(nb, nc), in_specs=in_specs,
        out_specs=(seq(1, FOX_WIDTH), seq(1, FOX_WIDTH)),
        scratch_shapes=scratch)
    return pl.pallas_call(
        functools.partial(_decode_kernel, pages=pages), grid_spec=grid_spec,
        out_shape=(jax.ShapeDtypeStruct((nb, 1, FOX_WIDTH), F32), jax.ShapeDtypeStruct((nb, 1, FOX_WIDTH), F32)),
        compiler_params=_params("parallel", "arbitrary"), name="decode_attention",
    )(pt, *operands)


def _tail_kernel(x_ref, fo_ref, mo_ref, p_ref, gfo_ref, gmo_ref, wo_ref, gffn_ref, wg_ref, wu_ref, wd_ref,
                 gple_ref, wpg_ref, wpp_ref, gfin_ref, y_ref):
    fo = _rms(fo_ref[...], gfo_ref[...]).astype(BF16)
    mo = _rms(mo_ref[...], gmo_ref[...]).astype(BF16)
    o = jnp.dot(fo, wo_ref[:FOX_WIDTH, :], preferred_element_type=F32)
    o = o + jnp.dot(mo, wo_ref[FOX_WIDTH:, :], preferred_element_type=F32)
    h = x_ref[...] + o

    hn = _rms(h, gffn_ref[...]).astype(BF16)
    g = jnp.dot(hn, wg_ref[...], preferred_element_type=F32)
    u = jnp.dot(hn, wu_ref[...], preferred_element_type=F32)
    act = (g * (1.0 / (1.0 + jnp.exp(-g))) * u).astype(BF16)
    h = h + jnp.dot(act, wd_ref[...], preferred_element_type=F32)

    hp = _rms(h, gple_ref[...]).astype(BF16)
    zg = jnp.dot(hp, wpg_ref[...], preferred_element_type=F32)
    gate = 1.0 / (1.0 + jnp.exp(-zg))
    proj = jnp.dot(p_ref[...].astype(BF16), wpp_ref[...], preferred_element_type=F32)
    h = h + gate * proj
    y_ref[...] = _rms(h, gfin_ref[...])


def _tail(x, fox_o, mla_o, p, W, bm):
    m = x.shape[0]
    row = lambda w_: pl.BlockSpec((bm, w_), lambda i: (i, 0))
    return pl.pallas_call(
        _tail_kernel, grid=(m // bm,),
        in_specs=[row(D_MODEL), row(FOX_WIDTH), row(FOX_WIDTH), row(PLE_DIM),
                  _const_spec((1, FOX_WIDTH)), _const_spec((1, FOX_WIDTH)), _const_spec((D_MODEL, D_MODEL)),
                  _const_spec((1, D_MODEL)), _const_spec((D_MODEL, D_FF)),
                  _const_spec((D_MODEL, D_FF)), _const_spec((D_FF, D_MODEL)), _const_spec((1, D_MODEL)),
                  _const_spec((D_MODEL, D_MODEL)), _const_spec((PLE_DIM, D_MODEL)), _const_spec((1, D_MODEL))],
        out_specs=row(D_MODEL), out_shape=jax.ShapeDtypeStruct((m, D_MODEL), F32),
        compiler_params=_params("parallel"), name="out_proj_ffn_ple_norm",
    )(x, fox_o, mla_o, p, W["g_fox_out"], W["g_mla_out"], W["w_o"], W["g_ffn"], W["w_gate"], W["w_up"], W["w_down"],
      W["g_ple"], W["w_ple_gate"], W["w_ple_proj"], W["g_final"])


def _prep_weights(g_attn, w_in, b_f, g_q, w_uq, g_kv, w_uk, w_uv, g_fox_out, g_mla_out, w_o, g_ffn, w_gate, w_up,
                  w_down, g_ple, w_ple_gate, w_ple_proj, g_final):
    w = w_in[0]
    o = np.cumsum((0, 512, 512, 512, N_HEADS, Q_LORA, KV_LORA, ROPE))
    fq, fk, fv, fl, qc, kv, kr = (w[:, o[i]:o[i + 1]] for i in range(7))
    kr_sw = jnp.concatenate([kr[:, ROPE // 2:], kr[:, :ROPE // 2]], axis=1)
    misc = jnp.concatenate([kr, kr_sw, fl, jnp.zeros((D_MODEL, 128 - 2 * ROPE - N_HEADS), F32)], axis=1)
    w_in_p = jnp.concatenate([fq, fk, fv, qc, kv, misc], axis=1).astype(BF16)

    uq = w_uq[0].reshape(Q_LORA, N_HEADS, NOPE + ROPE)
    uq_nope = uq[:, :, :NOPE].reshape(Q_LORA, N_HEADS * NOPE)
    uq_rope = uq[:, :, NOPE:]
    uq_rope_sw = jnp.concatenate([uq_rope[:, :, ROPE // 2:], uq_rope[:, :, :ROPE // 2]], axis=2)
    w_uq_p = jnp.concatenate([uq_nope, uq_rope.reshape(Q_LORA, -1), uq_rope_sw.reshape(Q_LORA, -1)], axis=1).astype(BF16)

    uk = jnp.transpose(w_uk[0], (1, 2, 0))
    zeros = jnp.zeros_like(uk)
    even = jnp.concatenate([uk, zeros], axis=1)
    odd = jnp.concatenate([zeros, uk], axis=1)
    w_uk_p = jnp.where((jnp.arange(N_HEADS) % 2 == 0)[:, None, None], even, odd).astype(BF16)

    b_row = jnp.zeros((1, 128), F32).at[0, M_FL:M_FL + N_HEADS].set(b_f[0])
    return dict(
        g_attn=g_attn, w_in=w_in_p, b_f=b_row, g_q=g_q, g_kv=g_kv, w_uq=w_uq_p, w_uk=w_uk_p,
        w_uv_t=jnp.transpose(w_uv[0], (1, 2, 0)).astype(BF16),
        w_uv_flat=w_uv[0].reshape(KV_LORA, FOX_WIDTH).astype(BF16),
        g_fox_out=g_fox_out, g_mla_out=g_mla_out,
        w_o=w_o[0].astype(BF16), g_ffn=g_ffn, w_gate=w_gate[0].astype(BF16), w_up=w_up[0].astype(BF16),
        w_down=w_down[0].astype(BF16), g_ple=g_ple, w_ple_gate=w_ple_gate[0].astype(BF16),
        w_ple_proj=w_ple_proj[0].astype(BF16), g_final=g_final.reshape(1, D_MODEL))


def _rope_tables(pos):
    half = ROPE // 2
    inv = ROPE_THETA ** (-jnp.arange(half, dtype=F32) / half)
    ang = pos.astype(F32)[:, None] * inv[None, :]
    cos, sin = jnp.cos(ang), jnp.sin(ang)
    c32 = jnp.concatenate([cos, cos], axis=1)
    s32 = jnp.concatenate([-sin, sin], axis=1)
    pad = jnp.zeros((pos.shape[0], 128 - 2 * ROPE), F32)
    return jnp.concatenate([jnp.tile(c32, (1, N_HEADS)), jnp.tile(s32, (1, N_HEADS)), c32, s32, pad], axis=1)


def _block(m, target):
    b = min(m, target)
    while m % b:
        b //= 2
    return b


def kernel(x_prompt, x_sample, cache_fox_k, cache_fox_v, cache_fox_logf, cache_mla_latent, page_table, p_prompt, p_sample, g_attn, w_in, b_f, g_q, w_uq, g_kv, w_uk, w_uv, g_fox_out, g_mla_out, w_o, g_ffn, w_gate, w_up, w_down, g_ple, w_ple_gate, w_ple_proj, g_final):
    assert w_in.shape[0] == 1, "one layer"
    W = _prep_weights(g_attn, w_in, b_f, g_q, w_uq, g_kv, w_uk, w_uv, g_fox_out, g_mla_out, w_o, g_ffn, w_gate,
                      w_up, w_down, g_ple, w_ple_gate, w_ple_proj, g_final)
    b, t, _ = x_prompt.shape
    nb, ts, _ = x_sample.shape
    assert ts == 1
    n_pool = cache_fox_k.shape[1]
    past_len = page_table.shape[1] * PAGE

    def heads_last(xt, lead):
        return jnp.transpose(xt.reshape(lead, N_HEADS, HEAD_DIM, -1), (0, 3, 1, 2))

    mp = b * t
    xp = x_prompt.reshape(mp, D_MODEL)
    tq = _block(t, 256)
    tab = _rope_tables(jnp.arange(t, dtype=jnp.int32))
    fq, fk, fkt, fvt, logft, latt, lattb, latr, qlat = _in_proj(xp, W, tab, b, t, tq)
    cp = _cumsum(logft)
    fox_o = _fox_attention(fq, fk, fvt, cp, tq)
    mla_o = _mla_attention(qlat, latr, lattb, W["w_uv_t"], tq)
    y_p = _tail(xp, fox_o, mla_o, p_prompt.reshape(mp, PLE_DIM), W, _block(mp, 256))

    xs = x_sample.reshape(nb, D_MODEL)
    tab_s = _rope_tables(jnp.full((nb,), past_len, jnp.int32))
    fq_s, fk_s, fkt_s, fvt_s, logft_s, latt_s, _, _, qlat_s = _in_proj(xs, W, tab_s, 1, nb, nb)
    fox_s, mla_s = _decode_attention(
        page_table, fq_s, jnp.transpose(qlat_s, (1, 0, 2)), fk_s, fvt_s[0].T, latt_s[0].T, logft_s[0].T,
        W["w_uv_flat"],
        jnp.transpose(cache_fox_k[0], (0, 2, 3, 1)).reshape(n_pool, FOX_WIDTH, PAGE),
        jnp.transpose(cache_fox_v[0], (0, 2, 3, 1)).reshape(n_pool, FOX_WIDTH, PAGE),
        jnp.transpose(cache_mla_latent[0], (0, 2, 1)), jnp.transpose(cache_fox_logf[0], (0, 2, 1)),
        pages=_block(page_table.shape[1], 32))
    y_s = _tail(xs, fox_s.reshape(nb, FOX_WIDTH), mla_s.reshape(nb, FOX_WIDTH), p_sample.reshape(nb, PLE_DIM), W, nb)

    return (
        y_p.reshape(b, t, D_MODEL), y_s.reshape(nb, 1, D_MODEL),
        heads_last(fkt, b)[None], heads_last(fvt, b)[None],
        jnp.transpose(logft, (0, 2, 1))[None], jnp.transpose(latt, (0, 2, 1))[None],
        heads_last(fkt_s, 1).reshape(1, nb, 1, N_HEADS, HEAD_DIM), heads_last(fvt_s, 1).reshape(1, nb, 1, N_HEADS, HEAD_DIM),
        jnp.transpose(logft_s, (0, 2, 1)).reshape(1, nb, 1, N_HEADS),
        jnp.transpose(latt_s, (0, 2, 1)).reshape(1, nb, 1, LATENT),
    )
```

```python
import functools

import jax
import jax.numpy as jnp
import numpy as np
from jax import lax
from jax.experimental import pallas as pl
from jax.experimental.pallas import tpu as pltpu

D_MODEL = 1024
HEAD_DIM = 64
N_HEADS = 8
FOX_WIDTH = N_HEADS * HEAD_DIM
NOPE = 64
ROPE = 32
Q_LORA = 384
KV_LORA = 256
LATENT = KV_LORA + ROPE
D_FF = 2816
PLE_DIM = 256
PAGE = 128
ROPE_THETA = 10000.0
RMS_EPS = 1e-6
LOG2E = 1.4426950408889634
FOX_SCALE = HEAD_DIM ** -0.5 * LOG2E
MLA_SCALE = (NOPE + ROPE) ** -0.5 * LOG2E
NEG_BIG = -1e30
SKIP = -2e30

C_FQ, C_FK, C_FV, C_QC, C_KV, C_MISC = 0, 512, 1024, 1536, 1920, 2176
IN_COLS = 2304
M_KR, M_KRSW, M_FL = 0, 32, 64
TAB_COLS = 640
CP_COLS = 32

F32 = jnp.float32
BF16 = jnp.bfloat16
VMEM_LIMIT = 56 * 1024 * 1024

NT = (((1,), (1,)), ((), ()))


def _rms(x, g):
    return x * lax.rsqrt(jnp.mean(x * x, axis=-1, keepdims=True) + RMS_EPS) * g


def _const_spec(shape):
    nd = len(shape)
    return pl.BlockSpec(shape, lambda *_: (0,) * nd, pipeline_mode=pl.Buffered(1))


def _params(*sem):
    return pltpu.CompilerParams(dimension_semantics=sem, vmem_limit_bytes=VMEM_LIMIT)


def _lane_cumsum(x):
    n = x.shape[-1]
    lane = lax.broadcasted_iota(jnp.int32, x.shape, x.ndim - 1)
    sh = 1
    while sh < n:
        x = x + jnp.where(lane >= sh, pltpu.roll(x, sh, axis=x.ndim - 1), 0.0)
        sh *= 2
    return x


def _in_kernel(x_ref, gat_ref, win_ref, bf_ref, gq_ref, gkv_ref, wuq_ref, wuk_ref, tab_ref,
               fq_ref, fk_ref, fkt_ref, fvt_ref, logft_ref, latt_ref, lattb_ref, latr_ref, qlat_ref):
    x = x_ref[...]
    hn = _rms(x, gat_ref[...]).astype(BF16)
    z = jnp.dot(hn, win_ref[...], preferred_element_type=F32)

    fq_ref[...] = (z[:, C_FQ:C_FQ + FOX_WIDTH] * FOX_SCALE).astype(BF16)
    fk_ref[...] = z[:, C_FK:C_FK + FOX_WIDTH]
    fkt_ref[...] = z[:, C_FK:C_FK + FOX_WIDTH].T
    fvt_ref[...] = z[:, C_FV:C_FV + FOX_WIDTH].T

    misc = z[:, C_MISC:C_MISC + 128]
    t = misc + bf_ref[...]
    lf = jnp.minimum(t, 0.0) - jnp.log1p(jnp.exp(-jnp.abs(t)))
    lane = lax.broadcasted_iota(jnp.int32, misc.shape, 1)
    rot = misc * tab_ref[:, 512:640]
    mt = jnp.where(lane < M_FL, rot, lf).T
    k_rot_t = mt[M_KR:M_KR + ROPE] + mt[M_KRSW:M_KRSW + ROPE]
    logft_ref[...] = mt[M_FL:M_FL + N_HEADS]

    kvn = _rms(z[:, C_KV:C_KV + KV_LORA], gkv_ref[...])
    kvn_t = kvn.T
    latt_ref[:KV_LORA, :] = kvn_t
    latt_ref[KV_LORA:, :] = k_rot_t
    lattb_ref[:KV_LORA, :] = kvn_t.astype(BF16)
    lattb_ref[KV_LORA:, :] = k_rot_t.astype(BF16)
    latr_ref[:, :KV_LORA] = kvn.astype(BF16)
    latr_ref[:, KV_LORA:] = (rot[:, M_KR:M_KR + ROPE] + rot[:, M_KRSW:M_KRSW + ROPE]).astype(BF16)

    qn = _rms(z[:, C_QC:C_QC + Q_LORA], gq_ref[...]).astype(BF16)
    q = jnp.dot(qn, wuq_ref[...], preferred_element_type=F32)
    q_rot = q[:, 512:768] * tab_ref[:, 0:256] + q[:, 768:1024] * tab_ref[:, 256:512]
    q_rot = (q_rot * MLA_SCALE).astype(BF16)
    for h in range(N_HEADS):
        p = h // 2
        q_pair = q[:, p * 128:(p + 1) * 128].astype(BF16)
        ql = jnp.dot(q_pair, wuk_ref[h], preferred_element_type=F32)
        qlat_ref[h, :, :KV_LORA] = (ql * MLA_SCALE).astype(BF16)
        qlat_ref[h, :, KV_LORA:] = q_rot[:, h * ROPE:(h + 1) * ROPE]


def _in_proj(x, W, tab, b, t, bm):
    m = b * t
    nt = t // bm
    row = lambda w: pl.BlockSpec((bm, w), lambda i: (i, 0))
    tr = lambda w: pl.BlockSpec((None, w, bm), lambda i: (i // nt, 0, i % nt))
    out_shape = (
        jax.ShapeDtypeStruct((m, FOX_WIDTH), BF16),
        jax.ShapeDtypeStruct((m, FOX_WIDTH), F32),
        jax.ShapeDtypeStruct((b, FOX_WIDTH, t), F32),
        jax.ShapeDtypeStruct((b, FOX_WIDTH, t), F32),
        jax.ShapeDtypeStruct((b, N_HEADS, t), F32),
        jax.ShapeDtypeStruct((b, LATENT, t), F32),
        jax.ShapeDtypeStruct((b, LATENT, t), BF16),
        jax.ShapeDtypeStruct((m, LATENT), BF16),
        jax.ShapeDtypeStruct((N_HEADS, m, LATENT), BF16),
    )
    out_specs = (row(FOX_WIDTH), row(FOX_WIDTH), tr(FOX_WIDTH), tr(FOX_WIDTH), tr(N_HEADS), tr(LATENT), tr(LATENT),
                 row(LATENT), pl.BlockSpec((N_HEADS, bm, LATENT), lambda i: (0, i, 0)))
    in_specs = [
        row(D_MODEL), _const_spec((1, D_MODEL)), _const_spec((D_MODEL, IN_COLS)), _const_spec((1, 128)),
        _const_spec((1, Q_LORA)), _const_spec((1, KV_LORA)), _const_spec((Q_LORA, 1024)),
        _const_spec((N_HEADS, 128, KV_LORA)), pl.BlockSpec((bm, TAB_COLS), lambda i: (i % nt, 0)),
    ]
    return pl.pallas_call(
        _in_kernel, grid=(m // bm,), in_specs=in_specs, out_specs=out_specs, out_shape=out_shape,
        compiler_params=_params("parallel"), name="in_proj",
    )(x, W["g_attn"], W["w_in"], W["b_f"], W["g_q"], W["g_kv"], W["w_uq"], W["w_uk"], tab)


def _cumsum_kernel(x_ref, o_ref):
    c = _lane_cumsum(x_ref[...]) * LOG2E
    hi = c.astype(BF16).astype(F32)
    r1 = c - hi
    mid = r1.astype(BF16).astype(F32)
    lo = (r1 - mid).astype(BF16).astype(F32)
    pieces = jnp.concatenate([hi, mid, lo, jnp.zeros((128 - 3 * N_HEADS, c.shape[1]), F32)], axis=0)
    o_ref[...] = pieces.T[:, :CP_COLS]


def _cumsum(logft):
    b, _, t = logft.shape
    return pl.pallas_call(
        _cumsum_kernel, grid=(b,),
        in_specs=[pl.BlockSpec((None, N_HEADS, t), lambda i: (i, 0, 0))],
        out_specs=pl.BlockSpec((t, CP_COLS), lambda i: (i, 0)),
        out_shape=jax.ShapeDtypeStruct((b * t, CP_COLS), F32),
        compiler_params=_params("parallel"), name="logf_cumsum",
    )(logft)


def _softmax_step_t(st, pv, m, l, acc):
    m_new = jnp.maximum(m, jnp.max(st, axis=0, keepdims=True))
    alpha = jnp.exp2(m - m_new)
    p = jnp.exp2(st - m_new)
    l = alpha * l + jnp.sum(p, axis=0, keepdims=True)
    acc = alpha * acc + pv(p)
    return m_new, l, acc


def _causal_mask_t(st, tq):
    key = lax.broadcasted_iota(jnp.int32, st.shape, 0)
    qry = lax.broadcasted_iota(jnp.int32, st.shape, 1) % tq
    return jnp.where(key <= qry, st, NEG_BIG)


def _causal_blocks(qi, tq, scores, update):
    def single(j, c):
        update(j, scores(j))
        return c

    def pair(i, c):
        ja = odd + 2 * i
        sa, sb = scores(ja), scores(ja + 1)
        update(ja, sa)
        update(ja + 1, sb)
        return c

    odd = qi % 2
    lax.fori_loop(0, odd, single, 0)
    lax.fori_loop(0, qi // 2, pair, 0)
    update(qi, tuple(_causal_mask_t(s, tq) for s in scores(qi)))


N_PAIRS = N_HEADS // 2
MLA_GROUPS = 4


def _fox_kernel(q_ref, k_ref, vt_ref, cp_ref, o_ref, m_scr, acc_scr, *, tq):
    qi = pl.program_id(1)
    r = 2 * tq
    lane = lax.broadcasted_iota(jnp.int32, (tq, 128), 1)
    xl = lax.broadcasted_iota(jnp.int32, (r, CP_COLS), 1)
    xr = lax.broadcasted_iota(jnp.int32, (r, CP_COLS), 0) // tq
    q2s = []
    for pair in range(N_PAIRS):
        q = q_ref[:, pair * 128:(pair + 1) * 128].astype(F32)
        q2 = jnp.concatenate([jnp.where(lane < HEAD_DIM, q, 0.0), jnp.where(lane >= HEAD_DIM, q, 0.0)], axis=0)
        qx = jnp.where((xl % N_HEADS == 2 * pair + xr) & (xl < 3 * N_HEADS), -1.0, 0.0)
        pad = jnp.zeros((r, 128 - CP_COLS), F32)
        q2s.append(jnp.concatenate([q2, qx, pad], axis=1))

    def scores(j):
        off = pl.multiple_of(j * tq, tq)
        cpb = cp_ref[pl.ds(off, tq), :]
        kpad = jnp.zeros((tq, 128 - CP_COLS), F32)
        sts = []
        for pair in range(N_PAIRS):
            kaug = jnp.concatenate([k_ref[pl.ds(off, tq), pair * 128:(pair + 1) * 128], cpb, kpad], axis=1)
            sts.append(lax.dot_general(kaug, q2s[pair], NT, preferred_element_type=F32))
        return tuple(sts)

    def update(j, sts):
        off = pl.multiple_of(j * tq, tq)
        ones = jnp.ones((8, tq), F32)
        for pair in range(N_PAIRS):
            vaug = jnp.concatenate([vt_ref[pair * 128:(pair + 1) * 128, pl.ds(off, tq)], ones], axis=0)
            m = m_scr[pair]
            m_new = jnp.maximum(m, jnp.max(sts[pair], axis=0, keepdims=True))
            p = jnp.exp2(sts[pair] - m_new)
            acc_scr[pair] = jnp.exp2(m - m_new) * acc_scr[pair] + jnp.dot(vaug, p, preferred_element_type=F32)
            m_scr[pair] = m_new

    m_scr[...] = jnp.full(m_scr.shape, NEG_BIG, F32)
    acc_scr[...] = jnp.zeros(acc_scr.shape, F32)
    _causal_blocks(qi, tq, scores, update)
    sub = lax.broadcasted_iota(jnp.int32, (128, tq), 0)
    for pair in range(N_PAIRS):
        acc = acc_scr[pair]
        ot = acc[:128] / acc[128:129]
        o_ref[:, pair * 128:(pair + 1) * 128] = jnp.where(sub < HEAD_DIM, ot[:, :tq], ot[:, tq:]).T


def _fox_attention(fq, fk, fvt, cp, tq):
    b, _, t = fvt.shape
    nq = t // tq
    return pl.pallas_call(
        functools.partial(_fox_kernel, tq=tq), grid=(b, nq),
        in_specs=[pl.BlockSpec((tq, FOX_WIDTH), lambda bi, qi: (bi * nq + qi, 0)),
                  pl.BlockSpec((t, FOX_WIDTH), lambda bi, qi: (bi, 0)),
                  pl.BlockSpec((None, FOX_WIDTH, t), lambda bi, qi: (bi, 0, 0)),
                  pl.BlockSpec((t, CP_COLS), lambda bi, qi: (bi, 0))],
        out_specs=pl.BlockSpec((tq, FOX_WIDTH), lambda bi, qi: (bi * nq + qi, 0)),
        out_shape=jax.ShapeDtypeStruct((b * t, FOX_WIDTH), F32),
        scratch_shapes=[pltpu.VMEM((N_PAIRS, 1, 2 * tq), F32), pltpu.VMEM((N_PAIRS, 136, 2 * tq), F32)],
        compiler_params=_params("parallel", "arbitrary"), name="fox_attention",
    )(fq, fk, fvt, cp)


def _mla_kernel(q_ref, latr_ref, latt_ref, wuvt_ref, o_ref, m_scr, l_scr, acc_scr, *, tq):
    qi = pl.program_id(1)
    hg = N_HEADS // MLA_GROUPS
    r = hg * tq
    qs = []
    for g in range(MLA_GROUPS):
        q = q_ref[g * hg:(g + 1) * hg].reshape(r, LATENT).astype(F32)
        q_rope = jnp.concatenate([q[:, KV_LORA:], jnp.zeros((r, 128 - ROPE), F32)], axis=1)
        qs.append(jnp.concatenate([q[:, :KV_LORA].T, q_rope.T[:ROPE]], axis=0).astype(BF16))

    def scores(j):
        off = pl.multiple_of(j * tq, tq)
        return tuple(jnp.dot(latr_ref[pl.ds(off, tq), :], qs[g], preferred_element_type=F32)
                     for g in range(MLA_GROUPS))

    def update(j, sts):
        off = pl.multiple_of(j * tq, tq)
        pv = lambda p: jnp.dot(latt_ref[:KV_LORA, pl.ds(off, tq)], p.astype(BF16), preferred_element_type=F32)
        for g in range(MLA_GROUPS):
            m_scr[g], l_scr[g], acc_scr[g] = _softmax_step_t(sts[g], pv, m_scr[g], l_scr[g], acc_scr[g])

    m_scr[...] = jnp.full(m_scr.shape, NEG_BIG, F32)
    l_scr[...] = jnp.zeros(l_scr.shape, F32)
    acc_scr[...] = jnp.zeros(acc_scr.shape, F32)
    _causal_blocks(qi, tq, scores, update)
    mo_t = []
    for g in range(MLA_GROUPS):
        ot = (acc_scr[g] / l_scr[g]).astype(BF16)
        for i in range(hg):
            mo_t.append(jnp.dot(wuvt_ref[g * hg + i], ot[:, i * tq:(i + 1) * tq], preferred_element_type=F32))
    o_ref[...] = jnp.concatenate(mo_t, axis=0).T


def _mla_attention(qlat, latr, lattb, wuvt, tq):
    b, _, t = lattb.shape
    nq = t // tq
    return pl.pallas_call(
        functools.partial(_mla_kernel, tq=tq), grid=(b, nq),
        in_specs=[pl.BlockSpec((N_HEADS, tq, LATENT), lambda bi, qi: (0, bi * nq + qi, 0)),
                  pl.BlockSpec((t, LATENT), lambda bi, qi: (bi, 0)),
                  pl.BlockSpec((None, LATENT, t), lambda bi, qi: (bi, 0, 0)),
                  _const_spec((N_HEADS, HEAD_DIM, KV_LORA))],
        out_specs=pl.BlockSpec((tq, FOX_WIDTH), lambda bi, qi: (bi * nq + qi, 0)),
        out_shape=jax.ShapeDtypeStruct((b * t, FOX_WIDTH), F32),
        scratch_shapes=[pltpu.VMEM((MLA_GROUPS, 1, N_HEADS // MLA_GROUPS * tq), F32),
                        pltpu.VMEM((MLA_GROUPS, 1, N_HEADS // MLA_GROUPS * tq), F32),
                        pltpu.VMEM((MLA_GROUPS, KV_LORA, N_HEADS // MLA_GROUPS * tq), F32)],
        compiler_params=_params("parallel", "arbitrary"), name="mla_attention",
    )(qlat, latr, lattb, wuvt)


def _softmax_step(s, pv, m, l, acc):
    m_new = jnp.maximum(m, jnp.max(s, axis=1, keepdims=True))
    alpha = jnp.exp2(m - m_new)
    p = jnp.exp2(s - m_new)
    l = alpha * l + jnp.sum(p, axis=1, keepdims=True)
    acc = alpha * acc + pv(p)
    return m_new, l, acc


def _decode_kernel(pt_ref, fq_ref, ql_ref, kn_ref, vn_ref, ln_ref, lfn_ref, wuv_ref, *rest, pages):
    kt_refs = rest[0 * pages:1 * pages]
    vt_refs = rest[1 * pages:2 * pages]
    lt_refs = rest[2 * pages:3 * pages]
    ft_refs = rest[3 * pages:4 * pages]
    fo_ref, mo_ref = rest[4 * pages:4 * pages + 2]
    mf, lf, af, mm, lm, am, coff = rest[4 * pages + 2:]
    c = pl.program_id(1)
    nc = pl.num_programs(1)

    @pl.when(c == 0)
    def _():
        mf[...] = jnp.full(mf.shape, NEG_BIG, F32)
        mm[...] = jnp.full(mm.shape, NEG_BIG, F32)
        for ref in (lf, af, lm, am, coff):
            ref[...] = jnp.zeros(ref.shape, F32)

    hrow = lax.broadcasted_iota(jnp.int32, (N_HEADS, FOX_WIDTH), 0)
    hcol = lax.broadcasted_iota(jnp.int32, (N_HEADS, FOX_WIDTH), 1) // HEAD_DIM
    own = hrow == hcol
    qbd = jnp.where(own, fq_ref[0].astype(F32), 0.0)
    qlat = ql_ref[0].astype(F32)
    cat = lambda refs: jnp.concatenate([r_[0] for r_ in refs], axis=1)

    kt = cat(kt_refs)
    lt = cat(lt_refs)
    s = jnp.dot(qbd, kt, preferred_element_type=F32)
    s2 = jnp.dot(qlat, lt, preferred_element_type=F32)

    cum = _lane_cumsum(cat(ft_refs)) + coff[:, 0:1]
    tk = cum.shape[1]
    c_last = cum[:, tk - 1:tk]
    coff[:, 0:1] = c_last

    vt = cat(vt_refs)
    pv = lambda p: lax.dot_general(p, vt, NT, preferred_element_type=F32)
    m_new, l_new, a_new = _softmax_step(s - cum * LOG2E, pv, mf[:, 0:1], lf[:, 0:1], af[...])
    pv2 = lambda p: lax.dot_general(p, lt[:KV_LORA], NT, preferred_element_type=F32)
    m2, l2, a2 = _softmax_step(s2, pv2, mm[:, 0:1], lm[:, 0:1], am[...])

    @pl.when(c < nc - 1)
    def _():
        mf[:, 0:1] = m_new
        lf[:, 0:1] = l_new
        af[...] = a_new
        mm[:, 0:1] = m2
        lm[:, 0:1] = l2
        am[...] = a2

    @pl.when(c == nc - 1)
    def _():
        c_tot = c_last + lfn_ref[0]
        sn = jnp.sum(qbd * kn_ref[0], axis=1, keepdims=True) - c_tot * LOG2E
        mfin = jnp.maximum(m_new, sn)
        al = jnp.exp2(m_new - mfin)
        pn = jnp.exp2(sn - mfin)
        o = (al * a_new + pn * vn_ref[0]) / (al * l_new + pn)
        fo_ref[0] = jnp.sum(jnp.where(own, o, 0.0), axis=0, keepdims=True)

        ln = ln_ref[0]
        sn2 = jnp.sum(qlat * ln, axis=1, keepdims=True)
        mfin2 = jnp.maximum(m2, sn2)
        al2 = jnp.exp2(m2 - mfin2)
        pn2 = jnp.exp2(sn2 - mfin2)
        o2 = ((al2 * a2 + pn2 * ln[:, :KV_LORA]) / (al2 * l2 + pn2)).astype(BF16)
        mo = jnp.dot(o2, wuv_ref[...], preferred_element_type=F32)
        mo_ref[0] = jnp.sum(jnp.where(own, mo, 0.0), axis=0, keepdims=True)


def _decode_attention(page_table, fq, qlat_t, fk_new, fv_new, lat_new, lf_new, wuv, ckt, cvt, clt, cft, pages):
    nb, n_pages = page_table.shape
    nc = n_pages // pages
    pt = page_table.reshape(-1)

    def page_spec(rows, i):
        return pl.BlockSpec((1, rows, PAGE), lambda b, c, pt_: (pt_[b * n_pages + c * pages + i], 0, 0))

    seq = lambda *shape: pl.BlockSpec((1,) + shape, lambda b, c, pt_: (b,) + (0,) * len(shape))
    in_specs = [seq(1, FOX_WIDTH), seq(N_HEADS, LATENT), seq(1, FOX_WIDTH), seq(1, FOX_WIDTH),
                seq(1, LATENT), seq(N_HEADS, 1), _const_spec((KV_LORA, FOX_WIDTH))]
    operands = [fq.reshape(nb, 1, FOX_WIDTH), qlat_t, fk_new.reshape(nb, 1, FOX_WIDTH),
                fv_new.reshape(nb, 1, FOX_WIDTH), lat_new.reshape(nb, 1, LATENT), lf_new.reshape(nb, N_HEADS, 1), wuv]
    for arr, rows in ((ckt, FOX_WIDTH), (cvt, FOX_WIDTH), (clt, LATENT), (cft, N_HEADS)):
        for i in range(pages):
            in_specs.append(page_spec(rows, i))
            operands.append(arr)
    scratch = [pltpu.VMEM((N_HEADS, 128), F32), pltpu.VMEM((N_HEADS, 128), F32), pltpu.VMEM((N_HEADS, FOX_WIDTH), F32),
               pltpu.VMEM((N_HEADS, 128), F32), pltpu.VMEM((N_HEADS, 128), F32), pltpu.VMEM((N_HEADS, KV_LORA), F32),
               pltpu.VMEM((N_HEADS, 128), F32)]
    grid_spec = pltpu.PrefetchScalarGridSpec(
        num_scalar_prefetch=1, grid=(nb, nc), in_specs=in_specs,
        out_specs=(seq(1, FOX_WIDTH), seq(1, FOX_WIDTH)),
        scratch_shapes=scratch)
    return pl.pallas_call(
        functools.partial(_decode_kernel, pages=pages), grid_spec=grid_spec,
        out_shape=(jax.ShapeDtypeStruct((nb, 1, FOX_WIDTH), F32), jax.ShapeDtypeStruct((nb, 1, FOX_WIDTH), F32)),
        compiler_params=_params("parallel", "arbitrary"), name="decode_attention",
    )(pt, *operands)


def _tail_kernel(x_ref, fo_ref, mo_ref, p_ref, gfo_ref, gmo_ref, wo_ref, gffn_ref, wg_ref, wu_ref, wd_ref,
                 gple_ref, wpg_ref, wpp_ref, gfin_ref, y_ref):
    fo = _rms(fo_ref[...], gfo_ref[...]).astype(BF16)
    mo = _rms(mo_ref[...], gmo_ref[...]).astype(BF16)
    o = jnp.dot(fo, wo_ref[:FOX_WIDTH, :], preferred_element_type=F32)
    o = o + jnp.dot(mo, wo_ref[FOX_WIDTH:, :], preferred_element_type=F32)
    h = x_ref[...] + o

    hn = _rms(h, gffn_ref[...]).astype(BF16)
    g = jnp.dot(hn, wg_ref[...], preferred_element_type=F32)
    u = jnp.dot(hn, wu_ref[...], preferred_element_type=F32)
    act = (g * (1.0 / (1.0 + jnp.exp(-g))) * u).astype(BF16)
    h = h + jnp.dot(act, wd_ref[...], preferred_element_type=F32)

    hp = _rms(h, gple_ref[...]).astype(BF16)
    zg = jnp.dot(hp, wpg_ref[...], preferred_element_type=F32)
    gate = 1.0 / (1.0 + jnp.exp(-zg))
    proj = jnp.dot(p_ref[...].astype(BF16), wpp_ref[...], preferred_element_type=F32)
    h = h + gate * proj
    y_ref[...] = _rms(h, gfin_ref[...])


def _tail(x, fox_o, mla_o, p, W, bm):
    m = x.shape[0]
    row = lambda w_: pl.BlockSpec((bm, w_), lambda i: (i, 0))
    return pl.pallas_call(
        _tail_kernel, grid=(m // bm,),
        in_specs=[row(D_MODEL), row(FOX_WIDTH), row(FOX_WIDTH), row(PLE_DIM),
                  _const_spec((1, FOX_WIDTH)), _const_spec((1, FOX_WIDTH)), _const_spec((D_MODEL, D_MODEL)),
                  _const_spec((1, D_MODEL)), _const_spec((D_MODEL, D_FF)),
                  _const_spec((D_MODEL, D_FF)), _const_spec((D_FF, D_MODEL)), _const_spec((1, D_MODEL)),
                  _const_spec((D_MODEL, D_MODEL)), _const_spec((PLE_DIM, D_MODEL)), _const_spec((1, D_MODEL))],
        out_specs=row(D_MODEL), out_shape=jax.ShapeDtypeStruct((m, D_MODEL), F32),
        compiler_params=_params("parallel"), name="out_proj_ffn_ple_norm",
    )(x, fox_o, mla_o, p, W["g_fox_out"], W["g_mla_out"], W["w_o"], W["g_ffn"], W["w_gate"], W["w_up"], W["w_down"],
      W["g_ple"], W["w_ple_gate"], W["w_ple_proj"], W["g_final"])


def _prep_weights(g_attn, w_in, b_f, g_q, w_uq, g_kv, w_uk, w_uv, g_fox_out, g_mla_out, w_o, g_ffn, w_gate, w_up,
                  w_down, g_ple, w_ple_gate, w_ple_proj, g_final):
    w = w_in[0]
    o = np.cumsum((0, 512, 512, 512, N_HEADS, Q_LORA, KV_LORA, ROPE))
    fq, fk, fv, fl, qc, kv, kr = (w[:, o[i]:o[i + 1]] for i in range(7))
    kr_sw = jnp.concatenate([kr[:, ROPE // 2:], kr[:, :ROPE // 2]], axis=1)
    misc = jnp.concatenate([kr, kr_sw, fl, jnp.zeros((D_MODEL, 128 - 2 * ROPE - N_HEADS), F32)], axis=1)
    w_in_p = jnp.concatenate([fq, fk, fv, qc, kv, misc], axis=1).astype(BF16)

    uq = w_uq[0].reshape(Q_LORA, N_HEADS, NOPE + ROPE)
    uq_nope = uq[:, :, :NOPE].reshape(Q_LORA, N_HEADS * NOPE)
    uq_rope = uq[:, :, NOPE:]
    uq_rope_sw = jnp.concatenate([uq_rope[:, :, ROPE // 2:], uq_rope[:, :, :ROPE // 2]], axis=2)
    w_uq_p = jnp.concatenate([uq_nope, uq_rope.reshape(Q_LORA, -1), uq_rope_sw.reshape(Q_LORA, -1)], axis=1).astype(BF16)

    uk = jnp.transpose(w_uk[0], (1, 2, 0))
    zeros = jnp.zeros_like(uk)
    even = jnp.concatenate([uk, zeros], axis=1)
    odd = jnp.concatenate([zeros, uk], axis=1)
    w_uk_p = jnp.where((jnp.arange(N_HEADS) % 2 == 0)[:, None, None], even, odd).astype(BF16)

    b_row = jnp.zeros((1, 128), F32).at[0, M_FL:M_FL + N_HEADS].set(b_f[0])
    return dict(
        g_attn=g_attn, w_in=w_in_p, b_f=b_row, g_q=g_q, g_kv=g_kv, w_uq=w_uq_p, w_uk=w_uk_p,
        w_uv_t=jnp.transpose(w_uv[0], (1, 2, 0)).astype(BF16),
        w_uv_flat=w_uv[0].reshape(KV_LORA, FOX_WIDTH).astype(BF16),
        g_fox_out=g_fox_out, g_mla_out=g_mla_out,
        w_o=w_o[0].astype(BF16), g_ffn=g_ffn, w_gate=w_gate[0].astype(BF16), w_up=w_up[0].astype(BF16),
        w_down=w_down[0].astype(BF16), g_ple=g_ple, w_ple_gate=w_ple_gate[0].astype(BF16),
        w_ple_proj=w_ple_proj[0].astype(BF16), g_final=g_final.reshape(1, D_MODEL))


def _rope_tables(pos):
    half = ROPE // 2
    inv = ROPE_THETA ** (-jnp.arange(half, dtype=F32) / half)
    ang = pos.astype(F32)[:, None] * inv[None, :]
    cos, sin = jnp.cos(ang), jnp.sin(ang)
    c32 = jnp.concatenate([cos, cos], axis=1)
    s32 = jnp.concatenate([-sin, sin], axis=1)
    pad = jnp.zeros((pos.shape[0], 128 - 2 * ROPE), F32)
    return jnp.concatenate([jnp.tile(c32, (1, N_HEADS)), jnp.tile(s32, (1, N_HEADS)), c32, s32, pad], axis=1)


def _block(m, target):
    b = min(m, target)
    while m % b:
        b //= 2
    return b


def kernel(x_prompt, x_sample, cache_fox_k, cache_fox_v, cache_fox_logf, cache_mla_latent, page_table, p_prompt, p_sample, g_attn, w_in, b_f, g_q, w_uq, g_kv, w_uk, w_uv, g_fox_out, g_mla_out, w_o, g_ffn, w_gate, w_up, w_down, g_ple, w_ple_gate, w_ple_proj, g_final):
    assert w_in.shape[0] == 1, "one layer"
    W = _prep_weights(g_attn, w_in, b_f, g_q, w_uq, g_kv, w_uk, w_uv, g_fox_out, g_mla_out, w_o, g_ffn, w_gate,
                      w_up, w_down, g_ple, w_ple_gate, w_ple_proj, g_final)
    b, t, _ = x_prompt.shape
    nb, ts, _ = x_sample.shape
    assert ts == 1
    n_pool = cache_fox_k.shape[1]
    past_len = page_table.shape[1] * PAGE

    def heads_last(xt, lead):
        return jnp.transpose(xt.reshape(lead, N_HEADS, HEAD_DIM, -1), (0, 3, 1, 2))

    mp = b * t
    xp = x_prompt.reshape(mp, D_MODEL)
    tq = _block(t, 256)
    tab = _rope_tables(jnp.arange(t, dtype=jnp.int32))
    fq, fk, fkt, fvt, logft, latt, lattb, latr, qlat = _in_proj(xp, W, tab, b, t, _block(t, 512))
    cp = _cumsum(logft)
    fox_o = _fox_attention(fq, fk, fvt, cp, tq)
    mla_o = _mla_attention(qlat, latr, lattb, W["w_uv_t"], tq)
    y_p = _tail(xp, fox_o, mla_o, p_prompt.reshape(mp, PLE_DIM), W, _block(mp, 512))

    xs = x_sample.reshape(nb, D_MODEL)
    tab_s = _rope_tables(jnp.full((nb,), past_len, jnp.int32))
    fq_s, fk_s, fkt_s, fvt_s, logft_s, latt_s, _, _, qlat_s = _in_proj(xs, W, tab_s, 1, nb, nb)
    fox_s, mla_s = _decode_attention(
        page_table, fq_s, jnp.transpose(qlat_s, (1, 0, 2)), fk_s, fvt_s[0].T, latt_s[0].T, logft_s[0].T,
        W["w_uv_flat"],
        jnp.transpose(cache_fox_k[0], (0, 2, 3, 1)).reshape(n_pool, FOX_WIDTH, PAGE),
        jnp.transpose(cache_fox_v[0], (0, 2, 3, 1)).reshape(n_pool, FOX_WIDTH, PAGE),
        jnp.transpose(cache_mla_latent[0], (0, 2, 1)), jnp.transpose(cache_fox_logf[0], (0, 2, 1)),
        pages=_block(page_table.shape[1], 32))
    y_s = _tail(xs, fox_s.reshape(nb, FOX_WIDTH), mla_s.reshape(nb, FOX_WIDTH), p_sample.reshape(nb, PLE_DIM), W, nb)

    return (
        y_p.reshape(b, t, D_MODEL), y_s.reshape(nb, 1, D_MODEL),
        heads_last(fkt, b)[None], heads_last(fvt, b)[None],
        jnp.transpose(logft, (0, 2, 1))[None], jnp.transpose(latt, (0, 2, 1))[None],
        heads_last(fkt_s, 1).reshape(1, nb, 1, N_HEADS, HEAD_DIM), heads_last(fvt_s, 1).reshape(1, nb, 1, N_HEADS, HEAD_DIM),
        jnp.transpose(logft_s, (0, 2, 1)).reshape(1, nb, 1, N_HEADS),
        jnp.transpose(latt_s, (0, 2, 1)).reshape(1, nb, 1, LATENT),
    )
```

```python
import functools

import jax
import jax.numpy as jnp
import numpy as np
from jax import lax
from jax.experimental import pallas as pl
from jax.experimental.pallas import tpu as pltpu

D_MODEL = 1024
HEAD_DIM = 64
N_HEADS = 8
FOX_WIDTH = N_HEADS * HEAD_DIM
NOPE = 64
ROPE = 32
Q_LORA = 384
KV_LORA = 256
LATENT = KV_LORA + ROPE
D_FF = 2816
PLE_DIM = 256
PAGE = 128
ROPE_THETA = 10000.0
RMS_EPS = 1e-6
LOG2E = 1.4426950408889634
FOX_SCALE = HEAD_DIM ** -0.5 * LOG2E
MLA_SCALE = (NOPE + ROPE) ** -0.5 * LOG2E
NEG_BIG = -1e30
SKIP = -2e30

C_FQ, C_FK, C_FV, C_QC, C_KV, C_MISC = 0, 512, 1024, 1536, 1920, 2176
IN_COLS = 2304
M_KR, M_KRSW, M_FL = 0, 32, 64
TAB_COLS = 640
CP_COLS = 32

F32 = jnp.float32
BF16 = jnp.bfloat16
VMEM_LIMIT = 56 * 1024 * 1024

NT = (((1,), (1,)), ((), ()))


def _rms(x, g):
    return x * lax.rsqrt(jnp.mean(x * x, axis=-1, keepdims=True) + RMS_EPS) * g


def _const_spec(shape):
    nd = len(shape)
    return pl.BlockSpec(shape, lambda *_: (0,) * nd, pipeline_mode=pl.Buffered(1))


def _params(*sem):
    return pltpu.CompilerParams(dimension_semantics=sem, vmem_limit_bytes=VMEM_LIMIT)


def _lane_cumsum(x):
    n = x.shape[-1]
    lane = lax.broadcasted_iota(jnp.int32, x.shape, x.ndim - 1)
    sh = 1
    while sh < n:
        x = x + jnp.where(lane >= sh, pltpu.roll(x, sh, axis=x.ndim - 1), 0.0)
        sh *= 2
    return x


def _in_kernel(x_ref, gat_ref, win_ref, bf_ref, gq_ref, gkv_ref, wuq_ref, wuk_ref, tab_ref,
               fq_ref, fk_ref, fkt_ref, fvt_ref, logft_ref, latt_ref, lattb_ref, latr_ref, qlat_ref):
    x = x_ref[...]
    hn = _rms(x, gat_ref[...]).astype(BF16)
    z = jnp.dot(hn, win_ref[...], preferred_element_type=F32)

    fq_ref[...] = (z[:, C_FQ:C_FQ + FOX_WIDTH] * FOX_SCALE).astype(BF16)
    fk_ref[...] = z[:, C_FK:C_FK + FOX_WIDTH]
    fkt_ref[...] = z[:, C_FK:C_FK + FOX_WIDTH].T
    fvt_ref[...] = z[:, C_FV:C_FV + FOX_WIDTH].T

    misc = z[:, C_MISC:C_MISC + 128]
    t = misc + bf_ref[...]
    lf = jnp.minimum(t, 0.0) - jnp.log1p(jnp.exp(-jnp.abs(t)))
    lane = lax.broadcasted_iota(jnp.int32, misc.shape, 1)
    rot = misc * tab_ref[:, 512:640]
    mt = jnp.where(lane < M_FL, rot, lf).T
    k_rot_t = mt[M_KR:M_KR + ROPE] + mt[M_KRSW:M_KRSW + ROPE]
    logft_ref[...] = mt[M_FL:M_FL + N_HEADS]

    kvn = _rms(z[:, C_KV:C_KV + KV_LORA], gkv_ref[...])
    kvn_t = kvn.T
    latt_ref[:KV_LORA, :] = kvn_t
    latt_ref[KV_LORA:, :] = k_rot_t
    lattb_ref[:KV_LORA, :] = kvn_t.astype(BF16)
    lattb_ref[KV_LORA:, :] = k_rot_t.astype(BF16)
    latr_ref[:, :KV_LORA] = kvn.astype(BF16)
    latr_ref[:, KV_LORA:] = (rot[:, M_KR:M_KR + ROPE] + rot[:, M_KRSW:M_KRSW + ROPE]).astype(BF16)

    qn = _rms(z[:, C_QC:C_QC + Q_LORA], gq_ref[...]).astype(BF16)
    q = jnp.dot(qn, wuq_ref[...], preferred_element_type=F32)
    q_rot = q[:, 512:768] * tab_ref[:, 0:256] + q[:, 768:1024] * tab_ref[:, 256:512]
    q_rot = (q_rot * MLA_SCALE).astype(BF16)
    for h in range(N_HEADS):
        p = h // 2
        q_pair = q[:, p * 128:(p + 1) * 128].astype(BF16)
        ql = jnp.dot(q_pair, wuk_ref[h], preferred_element_type=F32)
        qlat_ref[h, :, :KV_LORA] = (ql * MLA_SCALE).astype(BF16)
        qlat_ref[h, :, KV_LORA:] = q_rot[:, h * ROPE:(h + 1) * ROPE]


def _in_proj(x, W, tab, b, t, bm):
    m = b * t
    nt = t // bm
    row = lambda w: pl.BlockSpec((bm, w), lambda i: (i, 0))
    tr = lambda w: pl.BlockSpec((None, w, bm), lambda i: (i // nt, 0, i % nt))
    out_shape = (
        jax.ShapeDtypeStruct((m, FOX_WIDTH), BF16),
        jax.ShapeDtypeStruct((m, FOX_WIDTH), F32),
        jax.ShapeDtypeStruct((b, FOX_WIDTH, t), F32),
        jax.ShapeDtypeStruct((b, FOX_WIDTH, t), F32),
        jax.ShapeDtypeStruct((b, N_HEADS, t), F32),
        jax.ShapeDtypeStruct((b, LATENT, t), F32),
        jax.ShapeDtypeStruct((b, LATENT, t), BF16),
        jax.ShapeDtypeStruct((m, LATENT), BF16),
        jax.ShapeDtypeStruct((N_HEADS, m, LATENT), BF16),
    )
    out_specs = (row(FOX_WIDTH), row(FOX_WIDTH), tr(FOX_WIDTH), tr(FOX_WIDTH), tr(N_HEADS), tr(LATENT), tr(LATENT),
                 row(LATENT), pl.BlockSpec((N_HEADS, bm, LATENT), lambda i: (0, i, 0)))
    in_specs = [
        row(D_MODEL), _const_spec((1, D_MODEL)), _const_spec((D_MODEL, IN_COLS)), _const_spec((1, 128)),
        _const_spec((1, Q_LORA)), _const_spec((1, KV_LORA)), _const_spec((Q_LORA, 1024)),
        _const_spec((N_HEADS, 128, KV_LORA)), pl.BlockSpec((bm, TAB_COLS), lambda i: (i % nt, 0)),
    ]
    return pl.pallas_call(
        _in_kernel, grid=(m // bm,), in_specs=in_specs, out_specs=out_specs, out_shape=out_shape,
        compiler_params=_params("parallel"), name="in_proj",
    )(x, W["g_attn"], W["w_in"], W["b_f"], W["g_q"], W["g_kv"], W["w_uq"], W["w_uk"], tab)


def _cumsum_kernel(x_ref, o_ref):
    c = _lane_cumsum(x_ref[...]) * LOG2E
    hi = c.astype(BF16).astype(F32)
    r1 = c - hi
    mid = r1.astype(BF16).astype(F32)
    lo = (r1 - mid).astype(BF16).astype(F32)
    pieces = jnp.concatenate([hi, mid, lo, jnp.zeros((128 - 3 * N_HEADS, c.shape[1]), F32)], axis=0)
    o_ref[...] = pieces.T[:, :CP_COLS]


def _cumsum(logft):
    b, _, t = logft.shape
    return pl.pallas_call(
        _cumsum_kernel, grid=(b,),
        in_specs=[pl.BlockSpec((None, N_HEADS, t), lambda i: (i, 0, 0))],
        out_specs=pl.BlockSpec((t, CP_COLS), lambda i: (i, 0)),
        out_shape=jax.ShapeDtypeStruct((b * t, CP_COLS), F32),
        compiler_params=_params("parallel"), name="logf_cumsum",
    )(logft)


def _softmax_step_t(st, pv, m, l, acc):
    m_new = jnp.maximum(m, jnp.max(st, axis=0, keepdims=True))
    alpha = jnp.exp2(m - m_new)
    p = jnp.exp2(st - m_new)
    l = alpha * l + jnp.sum(p, axis=0, keepdims=True)
    acc = alpha * acc + pv(p)
    return m_new, l, acc


def _causal_mask_t(st, tq):
    key = lax.broadcasted_iota(jnp.int32, st.shape, 0)
    qry = lax.broadcasted_iota(jnp.int32, st.shape, 1) % tq
    return jnp.where(key <= qry, st, NEG_BIG)


STAGE_BLOCKS = 2


def _causal_blocks(qi, tq, scores, update):
    def single(j, c):
        update(j, scores(j))
        return c

    def stage(i, c):
        j0 = rem + STAGE_BLOCKS * i
        tiles = [scores(j0 + k) for k in range(STAGE_BLOCKS)]
        for k in range(STAGE_BLOCKS):
            update(j0 + k, tiles[k])
        return c

    rem = qi % STAGE_BLOCKS
    lax.fori_loop(0, rem, single, 0)
    lax.fori_loop(0, qi // STAGE_BLOCKS, stage, 0)
    update(qi, tuple(_causal_mask_t(s, tq) for s in scores(qi)))


N_PAIRS = N_HEADS // 2
MLA_GROUPS = 4


def _fox_kernel(q_ref, k_ref, vt_ref, cp_ref, o_ref, m_scr, acc_scr, *, tq):
    qi = pl.program_id(1)
    r = 2 * tq
    lane = lax.broadcasted_iota(jnp.int32, (tq, 128), 1)
    xl = lax.broadcasted_iota(jnp.int32, (r, CP_COLS), 1)
    xr = lax.broadcasted_iota(jnp.int32, (r, CP_COLS), 0) // tq
    q2s = []
    for pair in range(N_PAIRS):
        q = q_ref[:, pair * 128:(pair + 1) * 128].astype(F32)
        q2 = jnp.concatenate([jnp.where(lane < HEAD_DIM, q, 0.0), jnp.where(lane >= HEAD_DIM, q, 0.0)], axis=0)
        qx = jnp.where((xl % N_HEADS == 2 * pair + xr) & (xl < 3 * N_HEADS), -1.0, 0.0)
        pad = jnp.zeros((r, 128 - CP_COLS), F32)
        q2s.append(jnp.concatenate([q2, qx, pad], axis=1))

    def scores(j):
        off = pl.multiple_of(j * tq, tq)
        cpb = cp_ref[pl.ds(off, tq), :]
        kpad = jnp.zeros((tq, 128 - CP_COLS), F32)
        sts = []
        for pair in range(N_PAIRS):
            kaug = jnp.concatenate([k_ref[pl.ds(off, tq), pair * 128:(pair + 1) * 128], cpb, kpad], axis=1)
            sts.append(lax.dot_general(kaug, q2s[pair], NT, preferred_element_type=F32))
        return tuple(sts)

    def update(j, sts):
        off = pl.multiple_of(j * tq, tq)
        ones = jnp.ones((8, tq), F32)
        for pair in range(N_PAIRS):
            m = m_scr[pair]
            m_new = jnp.maximum(m, jnp.max(sts[pair], axis=0, keepdims=True))
            p = jnp.exp2(sts[pair] - m_new)
            alpha = jnp.exp2(m - m_new)
            for e in range(2):
                v0 = pair * 128 + e * HEAD_DIM
                vaug = jnp.concatenate([vt_ref[v0:v0 + HEAD_DIM, pl.ds(off, tq)], ones], axis=0)
                cols = slice(e * tq, (e + 1) * tq)
                acc_scr[pair, e] = alpha[:, cols] * acc_scr[pair, e] + jnp.dot(
                    vaug, p[:, cols], preferred_element_type=F32)
            m_scr[pair] = m_new

    m_scr[...] = jnp.full(m_scr.shape, NEG_BIG, F32)
    acc_scr[...] = jnp.zeros(acc_scr.shape, F32)
    _causal_blocks(qi, tq, scores, update)
    for pair in range(N_PAIRS):
        halves = []
        for e in range(2):
            acc = acc_scr[pair, e]
            halves.append(acc[:HEAD_DIM] / acc[HEAD_DIM:HEAD_DIM + 1])
        o_ref[:, pair * 128:(pair + 1) * 128] = jnp.concatenate(halves, axis=0).T


def _fox_attention(fq, fk, fvt, cp, tq):
    b, _, t = fvt.shape
    nq = t // tq
    return pl.pallas_call(
        functools.partial(_fox_kernel, tq=tq), grid=(b, nq),
        in_specs=[pl.BlockSpec((tq, FOX_WIDTH), lambda bi, qi: (bi * nq + qi, 0)),
                  pl.BlockSpec((t, FOX_WIDTH), lambda bi, qi: (bi, 0)),
                  pl.BlockSpec((None, FOX_WIDTH, t), lambda bi, qi: (bi, 0, 0)),
                  pl.BlockSpec((t, CP_COLS), lambda bi, qi: (bi, 0))],
        out_specs=pl.BlockSpec((tq, FOX_WIDTH), lambda bi, qi: (bi * nq + qi, 0)),
        out_shape=jax.ShapeDtypeStruct((b * t, FOX_WIDTH), F32),
        scratch_shapes=[pltpu.VMEM((N_PAIRS, 1, 2 * tq), F32), pltpu.VMEM((N_PAIRS, 2, HEAD_DIM + 8, tq), F32)],
        compiler_params=_params("parallel", "arbitrary"), name="fox_attention",
    )(fq, fk, fvt, cp)


def _mla_kernel(q_ref, latr_ref, latt_ref, wuvt_ref, o_ref, m_scr, l_scr, acc_scr, *, tq):
    qi = pl.program_id(1)
    hg = N_HEADS // MLA_GROUPS
    r = hg * tq
    qs = []
    for g in range(MLA_GROUPS):
        q = q_ref[g * hg:(g + 1) * hg].reshape(r, LATENT).astype(F32)
        q_rope = jnp.concatenate([q[:, KV_LORA:], jnp.zeros((r, 128 - ROPE), F32)], axis=1)
        qs.append(jnp.concatenate([q[:, :KV_LORA].T, q_rope.T[:ROPE]], axis=0).astype(BF16))

    def scores(j):
        off = pl.multiple_of(j * tq, tq)
        return tuple(jnp.dot(latr_ref[pl.ds(off, tq), :], qs[g], preferred_element_type=F32)
                     for g in range(MLA_GROUPS))

    def update(j, sts):
        off = pl.multiple_of(j * tq, tq)
        pv = lambda p: jnp.dot(latt_ref[:KV_LORA, pl.ds(off, tq)], p.astype(BF16), preferred_element_type=F32)
        for g in range(MLA_GROUPS):
            m_scr[g], l_scr[g], acc_scr[g] = _softmax_step_t(sts[g], pv, m_scr[g], l_scr[g], acc_scr[g])

    m_scr[...] = jnp.full(m_scr.shape, NEG_BIG, F32)
    l_scr[...] = jnp.zeros(l_scr.shape, F32)
    acc_scr[...] = jnp.zeros(acc_scr.shape, F32)
    _causal_blocks(qi, tq, scores, update)
    mo_t = []
    for g in range(MLA_GROUPS):
        ot = (acc_scr[g] / l_scr[g]).astype(BF16)
        for i in range(hg):
            mo_t.append(jnp.dot(wuvt_ref[g * hg + i], ot[:, i * tq:(i + 1) * tq], preferred_element_type=F32))
    o_ref[...] = jnp.concatenate(mo_t, axis=0).T


def _mla_attention(qlat, latr, lattb, wuvt, tq):
    b, _, t = lattb.shape
    nq = t // tq
    return pl.pallas_call(
        functools.partial(_mla_kernel, tq=tq), grid=(b, nq),
        in_specs=[pl.BlockSpec((N_HEADS, tq, LATENT), lambda bi, qi: (0, bi * nq + qi, 0)),
                  pl.BlockSpec((t, LATENT), lambda bi, qi: (bi, 0)),
                  pl.BlockSpec((None, LATENT, t), lambda bi, qi: (bi, 0, 0)),
                  _const_spec((N_HEADS, HEAD_DIM, KV_LORA))],
        out_specs=pl.BlockSpec((tq, FOX_WIDTH), lambda bi, qi: (bi * nq + qi, 0)),
        out_shape=jax.ShapeDtypeStruct((b * t, FOX_WIDTH), F32),
        scratch_shapes=[pltpu.VMEM((MLA_GROUPS, 1, N_HEADS // MLA_GROUPS * tq), F32),
                        pltpu.VMEM((MLA_GROUPS, 1, N_HEADS // MLA_GROUPS * tq), F32),
                        pltpu.VMEM((MLA_GROUPS, KV_LORA, N_HEADS // MLA_GROUPS * tq), F32)],
        compiler_params=_params("parallel", "arbitrary"), name="mla_attention",
    )(qlat, latr, lattb, wuvt)


def _softmax_step(s, pv, m, l, acc):
    m_new = jnp.maximum(m, jnp.max(s, axis=1, keepdims=True))
    alpha = jnp.exp2(m - m_new)
    p = jnp.exp2(s - m_new)
    l = alpha * l + jnp.sum(p, axis=1, keepdims=True)
    acc = alpha * acc + pv(p)
    return m_new, l, acc


def _decode_kernel(pt_ref, fq_ref, ql_ref, kn_ref, vn_ref, ln_ref, lfn_ref, wuv_ref, *rest, pages):
    kt_refs = rest[0 * pages:1 * pages]
    vt_refs = rest[1 * pages:2 * pages]
    lt_refs = rest[2 * pages:3 * pages]
    ft_refs = rest[3 * pages:4 * pages]
    fo_ref, mo_ref = rest[4 * pages:4 * pages + 2]
    mf, lf, af, mm, lm, am, coff = rest[4 * pages + 2:]
    c = pl.program_id(1)
    nc = pl.num_programs(1)

    @pl.when(c == 0)
    def _():
        mf[...] = jnp.full(mf.shape, NEG_BIG, F32)
        mm[...] = jnp.full(mm.shape, NEG_BIG, F32)
        for ref in (lf, af, lm, am, coff):
            ref[...] = jnp.zeros(ref.shape, F32)

    hrow = lax.broadcasted_iota(jnp.int32, (N_HEADS, FOX_WIDTH), 0)
    hcol = lax.broadcasted_iota(jnp.int32, (N_HEADS, FOX_WIDTH), 1) // HEAD_DIM
    own = hrow == hcol
    qbd = jnp.where(own, fq_ref[0].astype(F32), 0.0)
    qlat = ql_ref[0].astype(F32)
    cat = lambda refs: jnp.concatenate([r_[0] for r_ in refs], axis=1)

    kt = cat(kt_refs)
    lt = cat(lt_refs)
    s = jnp.dot(qbd, kt, preferred_element_type=F32)
    s2 = jnp.dot(qlat, lt, preferred_element_type=F32)

    cum = _lane_cumsum(cat(ft_refs)) + coff[:, 0:1]
    tk = cum.shape[1]
    c_last = cum[:, tk - 1:tk]
    coff[:, 0:1] = c_last

    vt = cat(vt_refs)
    pv = lambda p: lax.dot_general(p, vt, NT, preferred_element_type=F32)
    m_new, l_new, a_new = _softmax_step(s - cum * LOG2E, pv, mf[:, 0:1], lf[:, 0:1], af[...])
    pv2 = lambda p: lax.dot_general(p, lt[:KV_LORA], NT, preferred_element_type=F32)
    m2, l2, a2 = _softmax_step(s2, pv2, mm[:, 0:1], lm[:, 0:1], am[...])

    @pl.when(c < nc - 1)
    def _():
        mf[:, 0:1] = m_new
        lf[:, 0:1] = l_new
        af[...] = a_new
        mm[:, 0:1] = m2
        lm[:, 0:1] = l2
        am[...] = a2

    @pl.when(c == nc - 1)
    def _():
        c_tot = c_last + lfn_ref[0]
        sn = jnp.sum(qbd * kn_ref[0], axis=1, keepdims=True) - c_tot * LOG2E
        mfin = jnp.maximum(m_new, sn)
        al = jnp.exp2(m_new - mfin)
        pn = jnp.exp2(sn - mfin)
        o = (al * a_new + pn * vn_ref[0]) / (al * l_new + pn)
        fo_ref[0] = jnp.sum(jnp.where(own, o, 0.0), axis=0, keepdims=True)

        ln = ln_ref[0]
        sn2 = jnp.sum(qlat * ln, axis=1, keepdims=True)
        mfin2 = jnp.maximum(m2, sn2)
        al2 = jnp.exp2(m2 - mfin2)
        pn2 = jnp.exp2(sn2 - mfin2)
        o2 = ((al2 * a2 + pn2 * ln[:, :KV_LORA]) / (al2 * l2 + pn2)).astype(BF16)
        mo = jnp.dot(o2, wuv_ref[...], preferred_element_type=F32)
        mo_ref[0] = jnp.sum(jnp.where(own, mo, 0.0), axis=0, keepdims=True)


def _decode_attention(page_table, fq, qlat_t, fk_new, fv_new, lat_new, lf_new, wuv, ckt, cvt, clt, cft, pages):
    nb, n_pages = page_table.shape
    nc = n_pages // pages
    pt = page_table.reshape(-1)

    def page_spec(rows, i):
        return pl.BlockSpec((1, rows, PAGE), lambda b, c, pt_: (pt_[b * n_pages + c * pages + i], 0, 0))

    seq = lambda *shape: pl.BlockSpec((1,) + shape, lambda b, c, pt_: (b,) + (0,) * len(shape))
    in_specs = [seq(1, FOX_WIDTH), seq(N_HEADS, LATENT), seq(1, FOX_WIDTH), seq(1, FOX_WIDTH),
                seq(1, LATENT), seq(N_HEADS, 1), _const_spec((KV_LORA, FOX_WIDTH))]
    operands = [fq.reshape(nb, 1, FOX_WIDTH), qlat_t, fk_new.reshape(nb, 1, FOX_WIDTH),
                fv_new.reshape(nb, 1, FOX_WIDTH), lat_new.reshape(nb, 1, LATENT), lf_new.reshape(nb, N_HEADS, 1), wuv]
    for arr, rows in ((ckt, FOX_WIDTH), (cvt, FOX_WIDTH), (clt, LATENT), (cft, N_HEADS)):
        for i in range(pages):
            in_specs.append(page_spec(rows, i))
            operands.append(arr)
    scratch = [pltpu.VMEM((N_HEADS, 128), F32), pltpu.VMEM((N_HEADS, 128), F32), pltpu.VMEM((N_HEADS, FOX_WIDTH), F32),
               pltpu.VMEM((N_HEADS, 128), F32), pltpu.VMEM((N_HEADS, 128), F32), pltpu.VMEM((N_HEADS, KV_LORA), F32),
               pltpu.VMEM((N_HEADS, 128), F32)]
    grid_spec = pltpu.PrefetchScalarGridSpec(
        num_scalar_prefetch=1, grid=(nb, nc), in_specs=in_specs,
        out_specs=(seq(1, FOX_WIDTH), seq(1, FOX_WIDTH)),
        scratch_shapes=scratch)
    return pl.pallas_call(
        functools.partial(_decode_kernel, pages=pages), grid_spec=grid_spec,
        out_shape=(jax.ShapeDtypeStruct((nb, 1, FOX_WIDTH), F32), jax.ShapeDtypeStruct((nb, 1, FOX_WIDTH), F32)),
        compiler_params=_params("parallel", "arbitrary"), name="decode_attention",
    )(pt, *operands)


def _tail_kernel(x_ref, fo_ref, mo_ref, p_ref, gfo_ref, gmo_ref, wo_ref, gffn_ref, wg_ref, wu_ref, wd_ref,
                 gple_ref, wpg_ref, wpp_ref, gfin_ref, y_ref):
    fo = _rms(fo_ref[...], gfo_ref[...]).astype(BF16)
    mo = _rms(mo_ref[...], gmo_ref[...]).astype(BF16)
    o = jnp.dot(fo, wo_ref[:FOX_WIDTH, :], preferred_element_type=F32)
    o = o + jnp.dot(mo, wo_ref[FOX_WIDTH:, :], preferred_element_type=F32)
    h = x_ref[...] + o

    hn = _rms(h, gffn_ref[...]).astype(BF16)
    g = jnp.dot(hn, wg_ref[...], preferred_element_type=F32)
    u = jnp.dot(hn, wu_ref[...], preferred_element_type=F32)
    act = (g * (1.0 / (1.0 + jnp.exp(-g))) * u).astype(BF16)
    h = h + jnp.dot(act, wd_ref[...], preferred_element_type=F32)

    hp = _rms(h, gple_ref[...]).astype(BF16)
    zg = jnp.dot(hp, wpg_ref[...], preferred_element_type=F32)
    gate = 1.0 / (1.0 + jnp.exp(-zg))
    proj = jnp.dot(p_ref[...].astype(BF16), wpp_ref[...], preferred_element_type=F32)
    h = h + gate * proj
    y_ref[...] = _rms(h, gfin_ref[...])


def _tail(x, fox_o, mla_o, p, W, bm):
    m = x.shape[0]
    row = lambda w_: pl.BlockSpec((bm, w_), lambda i: (i, 0))
    return pl.pallas_call(
        _tail_kernel, grid=(m // bm,),
        in_specs=[row(D_MODEL), row(FOX_WIDTH), row(FOX_WIDTH), row(PLE_DIM),
                  _const_spec((1, FOX_WIDTH)), _const_spec((1, FOX_WIDTH)), _const_spec((D_MODEL, D_MODEL)),
                  _const_spec((1, D_MODEL)), _const_spec((D_MODEL, D_FF)),
                  _const_spec((D_MODEL, D_FF)), _const_spec((D_FF, D_MODEL)), _const_spec((1, D_MODEL)),
                  _const_spec((D_MODEL, D_MODEL)), _const_spec((PLE_DIM, D_MODEL)), _const_spec((1, D_MODEL))],
        out_specs=row(D_MODEL), out_shape=jax.ShapeDtypeStruct((m, D_MODEL), F32),
        compiler_params=_params("parallel"), name="out_proj_ffn_ple_norm",
    )(x, fox_o, mla_o, p, W["g_fox_out"], W["g_mla_out"], W["w_o"], W["g_ffn"], W["w_gate"], W["w_up"], W["w_down"],
      W["g_ple"], W["w_ple_gate"], W["w_ple_proj"], W["g_final"])


def _prep_weights(g_attn, w_in, b_f, g_q, w_uq, g_kv, w_uk, w_uv, g_fox_out, g_mla_out, w_o, g_ffn, w_gate, w_up,
                  w_down, g_ple, w_ple_gate, w_ple_proj, g_final):
    w = w_in[0]
    o = np.cumsum((0, 512, 512, 512, N_HEADS, Q_LORA, KV_LORA, ROPE))
    fq, fk, fv, fl, qc, kv, kr = (w[:, o[i]:o[i + 1]] for i in range(7))
    kr_sw = jnp.concatenate([kr[:, ROPE // 2:], kr[:, :ROPE // 2]], axis=1)
    misc = jnp.concatenate([kr, kr_sw, fl, jnp.zeros((D_MODEL, 128 - 2 * ROPE - N_HEADS), F32)], axis=1)
    w_in_p = jnp.concatenate([fq, fk, fv, qc, kv, misc], axis=1).astype(BF16)

    uq = w_uq[0].reshape(Q_LORA, N_HEADS, NOPE + ROPE)
    uq_nope = uq[:, :, :NOPE].reshape(Q_LORA, N_HEADS * NOPE)
    uq_rope = uq[:, :, NOPE:]
    uq_rope_sw = jnp.concatenate([uq_rope[:, :, ROPE // 2:], uq_rope[:, :, :ROPE // 2]], axis=2)
    w_uq_p = jnp.concatenate([uq_nope, uq_rope.reshape(Q_LORA, -1), uq_rope_sw.reshape(Q_LORA, -1)], axis=1).astype(BF16)

    uk = jnp.transpose(w_uk[0], (1, 2, 0))
    zeros = jnp.zeros_like(uk)
    even = jnp.concatenate([uk, zeros], axis=1)
    odd = jnp.concatenate([zeros, uk], axis=1)
    w_uk_p = jnp.where((jnp.arange(N_HEADS) % 2 == 0)[:, None, None], even, odd).astype(BF16)

    b_row = jnp.zeros((1, 128), F32).at[0, M_FL:M_FL + N_HEADS].set(b_f[0])
    return dict(
        g_attn=g_attn, w_in=w_in_p, b_f=b_row, g_q=g_q, g_kv=g_kv, w_uq=w_uq_p, w_uk=w_uk_p,
        w_uv_t=jnp.transpose(w_uv[0], (1, 2, 0)).astype(BF16),
        w_uv_flat=w_uv[0].reshape(KV_LORA, FOX_WIDTH).astype(BF16),
        g_fox_out=g_fox_out, g_mla_out=g_mla_out,
        w_o=w_o[0].astype(BF16), g_ffn=g_ffn, w_gate=w_gate[0].astype(BF16), w_up=w_up[0].astype(BF16),
        w_down=w_down[0].astype(BF16), g_ple=g_ple, w_ple_gate=w_ple_gate[0].astype(BF16),
        w_ple_proj=w_ple_proj[0].astype(BF16), g_final=g_final.reshape(1, D_MODEL))


def _rope_tables(pos):
    half = ROPE // 2
    inv = ROPE_THETA ** (-jnp.arange(half, dtype=F32) / half)
    ang = pos.astype(F32)[:, None] * inv[None, :]
    cos, sin = jnp.cos(ang), jnp.sin(ang)
    c32 = jnp.concatenate([cos, cos], axis=1)
    s32 = jnp.concatenate([-sin, sin], axis=1)
    pad = jnp.zeros((pos.shape[0], 128 - 2 * ROPE), F32)
    return jnp.concatenate([jnp.tile(c32, (1, N_HEADS)), jnp.tile(s32, (1, N_HEADS)), c32, s32, pad], axis=1)


def _block(m, target):
    b = min(m, target)
    while m % b:
        b //= 2
    return b


def kernel(x_prompt, x_sample, cache_fox_k, cache_fox_v, cache_fox_logf, cache_mla_latent, page_table, p_prompt, p_sample, g_attn, w_in, b_f, g_q, w_uq, g_kv, w_uk, w_uv, g_fox_out, g_mla_out, w_o, g_ffn, w_gate, w_up, w_down, g_ple, w_ple_gate, w_ple_proj, g_final):
    assert w_in.shape[0] == 1, "one layer"
    W = _prep_weights(g_attn, w_in, b_f, g_q, w_uq, g_kv, w_uk, w_uv, g_fox_out, g_mla_out, w_o, g_ffn, w_gate,
                      w_up, w_down, g_ple, w_ple_gate, w_ple_proj, g_final)
    b, t, _ = x_prompt.shape
    nb, ts, _ = x_sample.shape
    assert ts == 1
    n_pool = cache_fox_k.shape[1]
    past_len = page_table.shape[1] * PAGE

    def heads_last(xt, lead):
        return jnp.transpose(xt.reshape(lead, N_HEADS, HEAD_DIM, -1), (0, 3, 1, 2))

    mp = b * t
    xp = x_prompt.reshape(mp, D_MODEL)
    tq = _block(t, 256)
    tab = _rope_tables(jnp.arange(t, dtype=jnp.int32))
    fq, fk, fkt, fvt, logft, latt, lattb, latr, qlat = _in_proj(xp, W, tab, b, t, _block(t, 512))
    cp = _cumsum(logft)
    fox_o = _fox_attention(fq, fk, fvt, cp, tq)
    mla_o = _mla_attention(qlat, latr, lattb, W["w_uv_t"], tq)
    y_p = _tail(xp, fox_o, mla_o, p_prompt.reshape(mp, PLE_DIM), W, _block(mp, 512))

    xs = x_sample.reshape(nb, D_MODEL)
    tab_s = _rope_tables(jnp.full((nb,), past_len, jnp.int32))
    fq_s, fk_s, fkt_s, fvt_s, logft_s, latt_s, _, _, qlat_s = _in_proj(xs, W, tab_s, 1, nb, nb)
    fox_s, mla_s = _decode_attention(
        page_table, fq_s, jnp.transpose(qlat_s, (1, 0, 2)), fk_s, fvt_s[0].T, latt_s[0].T, logft_s[0].T,
        W["w_uv_flat"],
        jnp.transpose(cache_fox_k[0], (0, 2, 3, 1)).reshape(n_pool, FOX_WIDTH, PAGE),
        jnp.transpose(cache_fox_v[0], (0, 2, 3, 1)).reshape(n_pool, FOX_WIDTH, PAGE),
        jnp.transpose(cache_mla_latent[0], (0, 2, 1)), jnp.transpose(cache_fox_logf[0], (0, 2, 1)),
        pages=_block(page_table.shape[1], 32))
    y_s = _tail(xs, fox_s.reshape(nb, FOX_WIDTH), mla_s.reshape(nb, FOX_WIDTH), p_sample.reshape(nb, PLE_DIM), W, nb)

    return (
        y_p.reshape(b, t, D_MODEL), y_s.reshape(nb, 1, D_MODEL),
        heads_last(fkt, b)[None], heads_last(fvt, b)[None],
        jnp.transpose(logft, (0, 2, 1))[None], jnp.transpose(latt, (0, 2, 1))[None],
        heads_last(fkt_s, 1).reshape(1, nb, 1, N_HEADS, HEAD_DIM), heads_last(fvt_s, 1).reshape(1, nb, 1, N_HEADS, HEAD_DIM),
        jnp.transpose(logft_s, (0, 2, 1)).reshape(1, nb, 1, N_HEADS),
        jnp.transpose(latt_s, (0, 2, 1)).reshape(1, nb, 1, LATENT),
    )
```
